```python
import jax, jax.numpy as jnp
from jax import lax
import numpy as np

D_MODEL = 2048
BATCH = 2
SEQ = 8192
DEPTH = 2

F32 = jnp.float32
N_BRANCH = 4
W_BRANCH = D_MODEL // 2
LRU_BLOCKS = 8
LRU_BLOCK = W_BRANCH // LRU_BLOCKS
LRU_CONV = 4
LRU_C = 8.0
CONF_WIDTH = 31
POOL_WINDOWS = (2, 4, 8, 16)
POOL_GROUP = W_BRANCH // len(POOL_WINDOWS)
RWKV_HEAD = 64
RWKV_HEADS = W_BRANCH // RWKV_HEAD
DECAY_RANK = 64
AAA_RANK = 64
VMIX_RANK = 32
NORM_EPS = 1e-6
LN_EPS = 1e-5
GN_EPS = 64e-5
SPLITS = (W_BRANCH,) * 11 + (DECAY_RANK, AAA_RANK, N_BRANCH * D_MODEL)
N_IN = W_BRANCH * 11 + DECAY_RANK + AAA_RANK + N_BRANCH * D_MODEL

kernel_name = "hybrid_rglru_conformer_pool_rwkv7"


def _split_points():
    pts, acc = [], 0
    for s in SPLITS[:-1]:
        acc += s
        pts.append(acc)
    return pts


def rmsnorm(x, g):
    xf = x.astype(F32)
    y = xf * lax.rsqrt(jnp.mean(xf * xf, axis=-1, keepdims=True) + NORM_EPS)
    return (y * g.astype(F32)).astype(x.dtype)


def causal_dwconv(x, w, b):
    k = w.shape[0]
    y = lax.conv_general_dilated(
        x, w[:, None, :].astype(x.dtype), window_strides=(1,),
        padding=[(k - 1, 0)], dimension_numbers=("NWC", "WIO", "NWC"),
        feature_group_count=x.shape[-1])
    return y + b


def token_shift(s, mu):
    prev = jnp.pad(s, ((0, 0), (1, 0), (0, 0)))[:, :-1]
    return s + (prev - s) * mu


def rglru_branch(xa, conv_w, conv_b, wr, br, wi, bi, lam):
    xc = causal_dwconv(xa, conv_w, conv_b)
    b_, t_, _ = xc.shape
    xb = xc.reshape(b_, t_, LRU_BLOCKS, LRU_BLOCK)
    r = jax.nn.sigmoid(jnp.einsum("btnc,ncd->btnd", xb, wr).reshape(b_, t_, -1) + br)
    i = jax.nn.sigmoid(jnp.einsum("btnc,ncd->btnd", xb, wi).reshape(b_, t_, -1) + bi)
    log_a = (-LRU_C * r.astype(F32)) * jax.nn.softplus(-lam.astype(F32))
    a = jnp.exp(log_a)
    u = jnp.sqrt(-jnp.expm1(2.0 * log_a)) * (i * xc).astype(F32)

    def combine(left, right):
        a1, b1 = left
        a2, b2 = right
        return a1 * a2, a2 * b1 + b2

    _, h = lax.associative_scan(combine, (a, u), axis=1)
    return h.astype(xa.dtype)


def conformer_branch(v, glu, conv_w, conv_b, ln_g, ln_b):
    h = v * jax.nn.sigmoid(glu)
    h = causal_dwconv(h, conv_w, conv_b).astype(F32)
    mu = jnp.mean(h, axis=-1, keepdims=True)
    var = jnp.mean(jnp.square(h - mu), axis=-1, keepdims=True)
    h = (h - mu) * lax.rsqrt(var + LN_EPS) * ln_g.astype(F32) + ln_b.astype(F32)
    return jax.nn.silu(h).astype(v.dtype)


def pool_branch(xc, pool_w, pool_scale):
    b_, t_, _ = xc.shape
    xf = xc.astype(F32)
    count = jnp.arange(1, t_ + 1, dtype=F32)[None, :, None]
    outs = []
    for gi, w in enumerate(POOL_WINDOWS):
        xg = xf[..., gi * POOL_GROUP:(gi + 1) * POOL_GROUP]
        cs = jnp.cumsum(xg, axis=1)
        lower = jnp.pad(cs, ((0, 0), (w, 0), (0, 0)))[:, :t_]
        mean = (cs - lower) / jnp.minimum(count, float(w))
        outs.append(mean - xg)
    p = jnp.stack(outs, axis=2)
    y = jnp.einsum("btgc,gcd->btgd", p, pool_w.astype(F32)).reshape(b_, t_, -1)
    return (y * pool_scale.astype(F32)).astype(xc.dtype)


def rwkv7_scan(r, w, k, v, kk, a):
    b_, _, h_, n_ = r.shape

    def step(s, inp):
        r_t, w_t, k_t, v_t, kk_t, a_t = inp
        sa = jnp.einsum("bhvk,bhk->bhv", s, -kk_t)
        s = (s * w_t[:, :, None, :] + sa[..., None] * (kk_t * a_t)[:, :, None, :]
             + v_t[..., None] * k_t[:, :, None, :])
        return s, jnp.einsum("bhvk,bhk->bhv", s, r_t)

    xs = tuple(jnp.moveaxis(t, 1, 0) for t in (r, w, k, v, kk, a))
    s0 = jnp.zeros((b_, h_, n_, n_), F32)
    _, ys = lax.scan(step, s0, xs)
    return jnp.moveaxis(ys, 0, 1)


def rwkv7_branch(pr, pk, pv, zw, za, mu_rkv, mu_wa, w0, w2, a0, a2, k_k, k_a,
                 r_k, lnx_g, lnx_b, v_first, vmix):
    b_, t_, c_ = pr.shape
    r = token_shift(pr, mu_rkv[0])
    k = token_shift(pk, mu_rkv[1])
    v = token_shift(pv, mu_rkv[2])
    zw = token_shift(zw, mu_wa[0])
    za = token_shift(za, mu_wa[1])
    w_log = -jax.nn.softplus(-(w0 + jnp.tanh(zw) @ w2).astype(F32)) - 0.5
    decay = jnp.exp(-jnp.exp(w_log))
    if vmix is not None:
        v0, v1, v2 = vmix
        v = v + (v_first - v) * jax.nn.sigmoid(v0 + (v @ v1) @ v2)
    a = jax.nn.sigmoid((a0 + za @ a2).astype(F32))

    def heads(t):
        return t.astype(F32).reshape(b_, t_, RWKV_HEADS, RWKV_HEAD)

    kk = heads(k * k_k)
    kk = kk * lax.rsqrt(jnp.maximum(jnp.sum(kk * kk, axis=-1, keepdims=True), 1e-24))
    k_eff = k.astype(F32) * (1.0 + (a - 1.0) * k_a.astype(F32))
    rh, kh, vh, ah, wh = heads(r), heads(k_eff), heads(v), heads(a), heads(decay)
    y = rwkv7_scan(rh, wh, kh, vh, kk, ah)
    mu = jnp.mean(y, axis=-1, keepdims=True)
    var = jnp.mean(jnp.square(y - mu), axis=-1, keepdims=True)
    y = ((y - mu) * lax.rsqrt(var + GN_EPS)).reshape(b_, t_, c_)
    y = y * lnx_g.astype(F32) + lnx_b.astype(F32)
    bonus = jnp.sum(rh * kh * r_k.astype(F32), axis=-1, keepdims=True) * vh
    y = y + bonus.reshape(b_, t_, c_)
    return y.astype(pr.dtype), v


def setup_inputs(seed: int = 0) -> dict:
    key = jax.random.key(seed)
    ks = iter(jax.random.split(key, 40))

    def nrm(shape, scale):
        return scale * jax.random.normal(next(ks), shape, F32)

    L, D, W = DEPTH, D_MODEL, W_BRANCH
    lam_u = jax.random.uniform(next(ks), (L, W), F32, 0.9, 0.999)
    ramp = jnp.linspace(0.0, 1.0, W, dtype=F32)
    mu_rkv = jax.random.uniform(next(ks), (L, 3, W), F32)
    mu_wa = jax.random.uniform(next(ks), (L, 2, DECAY_RANK), F32)
    return {
        "x": nrm((BATCH, SEQ, D), 1.0),
        "norm_g": 1.0 + nrm((L, D), 0.02),
        "w_in": nrm((L, D, N_IN), D ** -0.5),
        "conv_a_w": nrm((L, LRU_CONV, W), LRU_CONV ** -0.5),
        "conv_a_b": nrm((L, W), 0.01),
        "lru_wr": nrm((L, LRU_BLOCKS, LRU_BLOCK, LRU_BLOCK), LRU_BLOCK ** -0.5),
        "lru_br": nrm((L, W), 0.01),
        "lru_wi": nrm((L, LRU_BLOCKS, LRU_BLOCK, LRU_BLOCK), LRU_BLOCK ** -0.5),
        "lru_bi": nrm((L, W), 0.01),
        "lru_lambda": jnp.log(lam_u) - jnp.log1p(-lam_u),
        "w_out_a": nrm((L, W, D), W ** -0.5),
        "conv_b_w": nrm((L, CONF_WIDTH, W), CONF_WIDTH ** -0.5),
        "conv_b_b": nrm((L, W), 0.01),
        "ln_b_g": 1.0 + nrm((L, W), 0.02),
        "ln_b_b": nrm((L, W), 0.01),
        "w_out_b": nrm((L, W, D), W ** -0.5),
        "pool_w": nrm((L, len(POOL_WINDOWS), POOL_GROUP, POOL_GROUP), POOL_GROUP ** -0.5),
        "pool_scale": 1.0 + nrm((L, W), 0.02),
        "w_out_c": nrm((L, W, D), W ** -0.5),
        "mu_rkv": mu_rkv,
        "mu_wa": mu_wa,
        "w0": -6.5 + 5.0 * ramp ** 0.85 + nrm((L, W), 0.1),
        "w2": nrm((L, DECAY_RANK, W), 0.5 * DECAY_RANK ** -0.5),
        "a0": nrm((L, W), 0.1),
        "a2": nrm((L, AAA_RANK, W), 0.5 * AAA_RANK ** -0.5),
        "k_k": 0.85 + nrm((L, W), 0.02),
        "k_a": 1.0 + nrm((L, W), 0.02),
        "r_k": nrm((L, RWKV_HEADS, RWKV_HEAD), 0.1),
        "lnx_g": 1.0 + nrm((L, W), 0.02),
        "lnx_b": nrm((L, W), 0.01),
        "v0": 1.0 + nrm((L - 1, W), 0.1),
        "v1": nrm((L - 1, W, VMIX_RANK), W ** -0.5),
        "v2": nrm((L - 1, VMIX_RANK, W), 0.5 * VMIX_RANK ** -0.5),
        "w_out_d": nrm((L, W, D), W ** -0.5),
        "w_o": nrm((L, D, D), D ** -0.5),
        "final_g": 1.0 + nrm((D,), 0.02),
    }


def reference(x, norm_g, w_in, conv_a_w, conv_a_b, lru_wr, lru_br, lru_wi, lru_bi,
              lru_lambda, w_out_a, conv_b_w, conv_b_b, ln_b_g, ln_b_b, w_out_b,
              pool_w, pool_scale, w_out_c, mu_rkv, mu_wa, w0, w2, a0, a2, k_k, k_a,
              r_k, lnx_g, lnx_b, v0, v1, v2, w_out_d, w_o, final_g):
    b_, t_, _ = x.shape
    pts = _split_points()
    v_first = None
    for l in range(DEPTH):
        u = rmsnorm(x, norm_g[l])
        (a_x, a_z, b_v, b_glu, b_z, c_x, c_z, d_r, d_k, d_v, d_z, d_zw, d_za,
         m) = jnp.split(u @ w_in[l], pts, axis=-1)
        h_a = rglru_branch(a_x, conv_a_w[l], conv_a_b[l], lru_wr[l], lru_br[l],
                           lru_wi[l], lru_bi[l], lru_lambda[l])
        h_b = conformer_branch(b_v, b_glu, conv_b_w[l], conv_b_b[l], ln_b_g[l], ln_b_b[l])
        h_c = pool_branch(c_x, pool_w[l], pool_scale[l])
        vmix = None if l == 0 else (v0[l - 1], v1[l - 1], v2[l - 1])
        h_d, v_l = rwkv7_branch(d_r, d_k, d_v, d_zw, d_za, mu_rkv[l], mu_wa[l], w0[l],
                                w2[l], a0[l], a2[l], k_k[l], k_a[l], r_k[l], lnx_g[l],
                                lnx_b[l], v_first, vmix)
        if l == 0:
            v_first = v_l
        gates = jax.nn.sigmoid(m.astype(F32)).astype(x.dtype).reshape(b_, t_, N_BRANCH, D_MODEL)
        merged = (gates[:, :, 0] * ((h_a * jax.nn.silu(a_z)) @ w_out_a[l])
                  + gates[:, :, 1] * ((h_b * jax.nn.silu(b_z)) @ w_out_b[l])
                  + gates[:, :, 2] * ((h_c * jax.nn.silu(c_z)) @ w_out_c[l])
                  + gates[:, :, 3] * ((h_d * jax.nn.silu(d_z)) @ w_out_d[l]))
        x = x + merged @ w_o[l]
    return rmsnorm(x, final_g)
```

```python
import functools

import jax
import jax.numpy as jnp
from jax import lax
from jax.experimental import pallas as pl
from jax.experimental.pallas import tpu as pltpu

F32 = jnp.float32
BF16 = jnp.bfloat16

D_MODEL = 2048
W = D_MODEL // 2
N_STREAM = 11
N_BRANCH = 4
LRU_BLOCKS = 8
LRU_CONV = 4
LRU_C = 8.0
CONF_WIDTH = 31
POOL_WINDOWS = (2, 4, 8, 16)
POOL_GROUP = W // len(POOL_WINDOWS)
HEAD = 64
LOW_RANK = 64
VMIX_RANK = 32
NORM_EPS = 1e-6
LN_EPS = 1e-5
GN_EPS = 64e-5

LANE = 128
SUBLANE = 8
CHUNK = 64
PAIR = 2 * HEAD
N_PAIR = W // PAIR
VMEM_LIMIT = 56 * 1024 * 1024

TIME_TILE = 256
ROW_BLOCK = 16


def _dot(a, b):
    return jnp.dot(a, b, preferred_element_type=F32)


def _dot_nt(a, b):
    return lax.dot_general(a, b, (((1,), (1,)), ((), ())), preferred_element_type=F32)


def _sigmoid(x):
    return 1.0 / (1.0 + jnp.exp(-x))


def _silu(x):
    return x * _sigmoid(x)


def _softplus(x):
    return jnp.maximum(x, 0.0) + jnp.log1p(jnp.exp(-jnp.abs(x)))


def _for_blocks(n_rows, rb, fn):
    def body(i, carry):
        fn(pl.multiple_of(i * rb, rb))
        return carry
    lax.fori_loop(0, n_rows // rb, body, 0)


def _params(n_grid):
    return pltpu.CompilerParams(dimension_semantics=("arbitrary",) * n_grid,
                                vmem_limit_bytes=VMEM_LIMIT)


def _shift_history(t, ext_ref, n_hist, n_rows):
    @pl.when(t == 0)
    def _():
        ext_ref[0:n_hist, :] = jnp.zeros((n_hist, ext_ref.shape[1]), ext_ref.dtype)

    @pl.when(t > 0)
    def _():
        ext_ref[0:n_hist, :] = ext_ref[n_rows:n_rows + n_hist, :]


def _inproj_kernel(x_ref, g_ref, w_ref, wwa_ref, z_ref, zwa_ref, u_ref):
    tm = x_ref.shape[0]

    @pl.when(pl.program_id(1) == 0)
    def _():
        g = g_ref[...]

        def blk(r0):
            x = x_ref[pl.ds(r0, 32), :]
            ms = jnp.mean(x * x, axis=-1, keepdims=True)
            u_ref[pl.ds(r0, 32), :] = (x * lax.rsqrt(ms + NORM_EPS) * g).astype(BF16)
        _for_blocks(tm, 32, blk)
        zwa_ref[...] = _dot(u_ref[...], wwa_ref[...])

    z_ref[...] = _dot(u_ref[...], w_ref[...])


def _inproj(x2, g, w_main, w_wa):
    m = x2.shape[0]
    n = w_main.shape[1]
    tm = min(1024, m)
    tn = 1024
    return pl.pallas_call(
        _inproj_kernel,
        grid=(m // tm, n // tn),
        in_specs=[
            pl.BlockSpec((tm, D_MODEL), lambda i, j: (i, 0)),
            pl.BlockSpec((1, D_MODEL), lambda i, j: (0, 0)),
            pl.BlockSpec((D_MODEL, tn), lambda i, j: (0, j)),
            pl.BlockSpec((D_MODEL, LANE), lambda i, j: (0, 0)),
        ],
        out_specs=[
            pl.BlockSpec((tm, tn), lambda i, j: (i, j)),
            pl.BlockSpec((tm, LANE), lambda i, j: (i, 0)),
        ],
        out_shape=[jax.ShapeDtypeStruct((m, n), F32), jax.ShapeDtypeStruct((m, LANE), F32)],
        scratch_shapes=[pltpu.VMEM((tm, D_MODEL), BF16)],
        compiler_params=_params(2),
        name="inproj",
    )(x2, g, w_main, w_wa)


def _lru_kernel(ax_ref, az_ref, cw_ref, cb_ref, wg_ref, bg_ref, lam_ref, o_ref,
                xe_ref, xc_ref, gt_ref, h_ref):
    t = pl.program_id(1)
    tt = ax_ref.shape[0]
    rb = ROW_BLOCK
    hist = SUBLANE
    _shift_history(t, xe_ref, hist, tt)

    @pl.when(t == 0)
    def _():
        h_ref[...] = jnp.zeros(h_ref.shape, F32)

    xe_ref[hist:hist + tt, :] = ax_ref[...]
    cw = cw_ref[...]
    cb = cb_ref[...]

    def conv_blk(r0):
        win = xe_ref[pl.ds(r0, rb + hist), :]
        acc = cb + cw[LRU_CONV - 1:LRU_CONV] * win[hist:hist + rb]
        for j in range(LRU_CONV - 1):
            off = hist - (LRU_CONV - 1) + j
            acc = acc + cw[j:j + 1] * win[off:off + rb]
        xc_ref[pl.ds(r0, rb), :] = acc
    _for_blocks(tt, rb, conv_blk)

    gt_ref[...] = _dot(xc_ref[...].astype(BF16), wg_ref[...]) + bg_ref[...]

    neg_c_softplus = -LRU_C * _softplus(-lam_ref[...])
    row = lax.broadcasted_iota(jnp.int32, (rb, W), 0) & (SUBLANE - 1)

    def scan_blk(i, h_prev):
        r0 = pl.multiple_of(i * rb, rb)
        xc = xc_ref[pl.ds(r0, rb), :]
        g = gt_ref[pl.ds(r0, rb), :]
        r = _sigmoid(g[:, :W])
        ig = _sigmoid(g[:, W:])
        log_a = neg_c_softplus * r
        a = jnp.exp(log_a)
        u = jnp.sqrt(-jnp.tanh(log_a) * (1.0 + a * a)) * (ig * xc)
        for s in (1, 2, 4):
            keep = row >= s
            a_s = jnp.where(keep, pltpu.roll(a, s, axis=0), 1.0)
            u_s = jnp.where(keep, pltpu.roll(u, s, axis=0), 0.0)
            u = u + a * u_s
            a = a * a_s
        hs = []
        for q in range(rb // SUBLANE):
            sl = slice(q * SUBLANE, (q + 1) * SUBLANE)
            h = u[sl] + a[sl] * h_prev
            h_prev = jnp.broadcast_to(h[SUBLANE - 1:SUBLANE], (SUBLANE, W))
            hs.append(h)
        h = jnp.concatenate(hs, axis=0)
        o_ref[pl.ds(r0, rb), :] = (h * _silu(az_ref[pl.ds(r0, rb), :])).astype(BF16)
        return h_prev

    h_ref[...] = lax.fori_loop(0, tt // rb, scan_blk, h_ref[...])


def _lru(z, nb, nt, cw, cb, wg, bg, lam):
    tt = TIME_TILE
    row = lambda c: pl.BlockSpec((tt, W), lambda b, t: (b * nt + t, c))
    full = lambda a: pl.BlockSpec(a.shape, lambda b, t: (0,) * a.ndim)
    return pl.pallas_call(
        _lru_kernel,
        grid=(nb, nt),
        in_specs=[row(0), row(1), full(cw), full(cb), full(wg), full(bg), full(lam)],
        out_specs=pl.BlockSpec((tt, W), lambda b, t: (b * nt + t, 0)),
        out_shape=jax.ShapeDtypeStruct((nb * nt * tt, W), BF16),
        scratch_shapes=[pltpu.VMEM((tt + SUBLANE, W), F32), pltpu.VMEM((tt, W), F32),
                        pltpu.VMEM((tt, 2 * W), F32), pltpu.VMEM((SUBLANE, W), F32)],
        compiler_params=_params(2),
        name="rglru",
    )(z, z, cw, cb, wg, bg, lam)


CONF_HIST = 32


def _conf_kernel(v_ref, glu_ref, bz_ref, cw_ref, cb_ref, lg_ref, lb_ref, o_ref, he_ref):
    t = pl.program_id(1)
    tt = v_ref.shape[0]
    rb = ROW_BLOCK
    _shift_history(t, he_ref, CONF_HIST, tt)

    def glu_blk(r0):
        he_ref[pl.ds(CONF_HIST + r0, rb), :] = (
            v_ref[pl.ds(r0, rb), :] * _sigmoid(glu_ref[pl.ds(r0, rb), :]))
    _for_blocks(tt, rb, glu_blk)

    cb = cb_ref[...]
    lg = lg_ref[...]
    lb = lb_ref[...]

    def conv_blk(r0):
        win = he_ref[pl.ds(r0, rb + CONF_HIST), :]
        acc = cb
        for j in range(CONF_WIDTH):
            off = CONF_HIST - (CONF_WIDTH - 1) + j
            acc = acc + cw_ref[j:j + 1, :] * win[off:off + rb]
        mu = jnp.mean(acc, axis=-1, keepdims=True)
        d = acc - mu
        var = jnp.mean(d * d, axis=-1, keepdims=True)
        y = d * lax.rsqrt(var + LN_EPS) * lg + lb
        o_ref[pl.ds(r0, rb), :] = (_silu(y) * _silu(bz_ref[pl.ds(r0, rb), :])).astype(BF16)
    _for_blocks(tt, rb, conv_blk)


def _conf(z, nb, nt, cw, cb, lg, lb):
    tt = TIME_TILE
    row = lambda c: pl.BlockSpec((tt, W), lambda b, t: (b * nt + t, c))
    full = lambda a: pl.BlockSpec(a.shape, lambda b, t: (0,) * a.ndim)
    return pl.pallas_call(
        _conf_kernel,
        grid=(nb, nt),
        in_specs=[row(2), row(3), row(4), full(cw), full(cb), full(lg), full(lb)],
        out_specs=pl.BlockSpec((tt, W), lambda b, t: (b * nt + t, 0)),
        out_shape=jax.ShapeDtypeStruct((nb * nt * tt, W), BF16),
        scratch_shapes=[pltpu.VMEM((tt + CONF_HIST, W), F32)],
        compiler_params=_params(2),
        name="conformer",
    )(z, z, z, cw, cb, lg, lb)


POOL_HIST = 16


def _pool_kernel(cx_ref, cz_ref, pw_ref, ps_ref, o_ref, xe_ref, p_ref, y_ref):
    t = pl.program_id(1)
    tt = cx_ref.shape[0]
    rb = ROW_BLOCK
    _shift_history(t, xe_ref, POOL_HIST, tt)
    xe_ref[POOL_HIST:POOL_HIST + tt, :] = cx_ref[...]
    t0 = t * tt

    def pool_blk(r0):
        win = xe_ref[pl.ds(r0, rb + POOL_HIST), :]
        count = (lax.broadcasted_iota(jnp.int32, (rb, POOL_GROUP), 0) + (t0 + r0 + 1)).astype(F32)
        for gi, w in enumerate(POOL_WINDOWS):
            lanes = slice(gi * POOL_GROUP, (gi + 1) * POOL_GROUP)
            wg = win[:, lanes]
            xg = wg[POOL_HIST:POOL_HIST + rb]
            s = xg
            for j in range(1, w):
                s = s + wg[POOL_HIST - j:POOL_HIST - j + rb]
            p_ref[pl.ds(r0, rb), lanes] = (s / jnp.minimum(count, float(w)) - xg).astype(BF16)
    _for_blocks(tt, rb, pool_blk)

    for gi in range(len(POOL_WINDOWS)):
        lanes = slice(gi * POOL_GROUP, (gi + 1) * POOL_GROUP)
        y_ref[:, lanes] = _dot(p_ref[:, lanes], pw_ref[gi])

    ps = ps_ref[...]

    def out_blk(r0):
        o_ref[pl.ds(r0, rb), :] = (
            y_ref[pl.ds(r0, rb), :] * ps * _silu(cz_ref[pl.ds(r0, rb), :])).astype(BF16)
    _for_blocks(tt, rb, out_blk)


def _pool(z, nb, nt, pw, ps):
    tt = TIME_TILE
    row = lambda c: pl.BlockSpec((tt, W), lambda b, t: (b * nt + t, c))
    full = lambda a: pl.BlockSpec(a.shape, lambda b, t: (0,) * a.ndim)
    return pl.pallas_call(
        _pool_kernel,
        grid=(nb, nt),
        in_specs=[row(5), row(6), full(pw), full(ps)],
        out_specs=pl.BlockSpec((tt, W), lambda b, t: (b * nt + t, 0)),
        out_shape=jax.ShapeDtypeStruct((nb * nt * tt, W), BF16),
        scratch_shapes=[pltpu.VMEM((tt + POOL_HIST, W), F32), pltpu.VMEM((tt, W), BF16),
                        pltpu.VMEM((tt, W), F32)],
        compiler_params=_params(2),
        name="pool",
    )(z, z, pw, ps)


def _split2(x):
    hi = x.astype(BF16)
    lo = (x - hi.astype(F32)).astype(BF16)
    return hi, lo


def _store_split(q_ref, r0, rb, x):
    hi, lo = _split2(x)
    for c in range(W // LANE):
        q_ref[pl.ds(r0, rb), 2 * c * LANE:(2 * c + 1) * LANE] = hi[:, c * LANE:(c + 1) * LANE]
        q_ref[pl.ds(r0, rb), (2 * c + 1) * LANE:(2 * c + 2) * LANE] = lo[:, c * LANE:(c + 1) * LANE]


def _head_sums(q_ref, ee_ref, out_ref):
    for c in range(W // LANE):
        out_ref[:, c * LANE:(c + 1) * LANE] = _dot(q_ref[:, 2 * c * LANE:(2 * c + 2) * LANE], ee_ref[...])


def _unit_lower_inverse(lmat, masks):
    eye, diag16, off32, off64 = masks
    d = jnp.where(diag16, lmat, 0.0)
    t = eye + d
    p = d.astype(BF16)
    for _ in range(3):
        p32 = _dot(p, p)
        p = p32.astype(BF16)
        t = t + _dot(t.astype(BF16), p)
    for off in (off32, off64):
        lo = jnp.where(off, lmat, 0.0).astype(BF16)
        tb = t.astype(BF16)
        t = t + _dot(tb, _dot(lo, tb).astype(BF16))
    return t


def _rwkv_kernel(has_vmix, *refs):
    if has_vmix:
        (pr_ref, pk_ref, pv_ref, dz_ref, zwa_ref, vf_ref, mu_ref, muwa_ref, w0_ref, wwa_ref, a0_ref,
         kk_ref, ka_ref, rk_ref, lg_ref, lb_ref, ee_ref, tri_ref, v0_ref, v1_ref, v2_ref,
         o_ref,
         pe_r, pe_k, pe_v, pe_wa, twa_s, wa_s, r_s, k_s, v_s, lw_s, kkn_s, be_s, q_s, ss_s,
         bs_s, y_s, cl_s, z_s, vm_s) = refs
        vout_ref = None
    else:
        (pr_ref, pk_ref, pv_ref, dz_ref, zwa_ref, mu_ref, muwa_ref, w0_ref, wwa_ref, a0_ref,
         kk_ref, ka_ref, rk_ref, lg_ref, lb_ref, ee_ref, tri_ref,
         o_ref, vout_ref,
         pe_r, pe_k, pe_v, pe_wa, twa_s, wa_s, r_s, k_s, v_s, lw_s, kkn_s, be_s, q_s, ss_s,
         bs_s, y_s, cl_s, z_s) = refs
    t = pl.program_id(1)
    tt = pr_ref.shape[0]
    rb = ROW_BLOCK
    hist = SUBLANE

    for ext in (pe_r, pe_k, pe_v, pe_wa):
        _shift_history(t, ext, hist, tt)

    @pl.when(t == 0)
    def _():
        z_s[...] = jnp.zeros(z_s.shape, F32)

    pe_r[hist:hist + tt, :] = pr_ref[...]
    pe_k[hist:hist + tt, :] = pk_ref[...]
    pe_v[hist:hist + tt, :] = pv_ref[...]
    pe_wa[hist:hist + tt, :] = zwa_ref[...]

    muwa = muwa_ref[...]
    lane_wa = lax.broadcasted_iota(jnp.int32, (rb, LANE), 1)

    def wa_blk(r0):
        win = pe_wa[pl.ds(r0, rb + hist), :]
        cur = win[hist:hist + rb]
        prev = win[hist - 1:hist - 1 + rb]
        s = cur + (prev - cur) * muwa
        twa_s[pl.ds(r0, rb), :] = jnp.where(lane_wa < LOW_RANK, jnp.tanh(s), s).astype(BF16)
    _for_blocks(tt, rb, wa_blk)
    wa_s[...] = _dot(twa_s[...], wwa_ref[...])

    k_k = kk_ref[...]

    def shift_blk(r0):
        for ext, idx, dst in ((pe_r, 0, r_s), (pe_k, 1, k_s), (pe_v, 2, v_s)):
            win = ext[pl.ds(r0, rb + hist), :]
            cur = win[hist:hist + rb]
            prev = win[hist - 1:hist - 1 + rb]
            dst[pl.ds(r0, rb), :] = cur + (prev - cur) * mu_ref[idx:idx + 1, :]
        kkr = k_s[pl.ds(r0, rb), :] * k_k
        _store_split(q_s, r0, rb, kkr * kkr)
    _for_blocks(tt, rb, shift_blk)
    _head_sums(q_s, ee_ref, ss_s)

    if has_vmix:
        vm_s[...] = _dot(_dot(v_s[...].astype(BF16), v1_ref[...]).astype(BF16), v2_ref[...])

    w0 = w0_ref[...]
    a0 = a0_ref[...]
    k_a = ka_ref[...]
    r_k = rk_ref[...]

    def prep_blk(r0):
        rows = pl.ds(r0, rb)
        wa = wa_s[rows, :]
        w_log = -_softplus(-(w0 + wa[:, :W])) - 0.5
        lw_s[rows, :] = -jnp.exp(w_log)
        a = _sigmoid(a0 + wa[:, W:])
        k = k_s[rows, :]
        kkn = (k * k_k) * lax.rsqrt(jnp.maximum(ss_s[rows, :], 1e-24))
        kkn_s[rows, :] = kkn
        be_s[rows, :] = kkn * a
        k_eff = k * (1.0 + (a - 1.0) * k_a)
        k_s[rows, :] = k_eff
        v = v_s[rows, :]
        if has_vmix:
            v = v + (vf_ref[rows, :] - v) * _sigmoid(v0_ref[...] + vm_s[rows, :])
            v_s[rows, :] = v
        else:
            vout_ref[rows, :] = v
        _store_split(q_s, r0, rb, r_s[rows, :] * k_eff * r_k)
    _for_blocks(tt, rb, prep_blk)
    _head_sums(q_s, ee_ref, bs_s)

    ri = lax.broadcasted_iota(jnp.int32, (PAIR, PAIR), 0)
    ci = lax.broadcasted_iota(jnp.int32, (PAIR, PAIR), 1)
    strict = ri > ci
    incl = ri >= ci
    eye = jnp.where(ri == ci, 1.0, 0.0).astype(F32)
    masks = (eye, (ri >> 4) == (ci >> 4),
             ((ri >> 5) == (ci >> 5)) & ((ri >> 4) != (ci >> 4)),
             ((ri >> 6) == (ci >> 6)) & ((ri >> 5) != (ci >> 5)))
    lane_c = lax.broadcasted_iota(jnp.int32, (CHUNK, PAIR), 1)
    first_head = lane_c < HEAD

    def stack(x):
        return jnp.concatenate([jnp.where(first_head, x, 0.0), jnp.where(first_head, 0.0, x)], axis=0)

    def chunk_body(c, carry):
        c0 = pl.multiple_of(c * CHUNK, CHUNK)
        rows = pl.ds(c0, CHUNK)
        lw = lw_s[rows, :]
        h1 = lw.astype(BF16)
        r1 = lw - h1.astype(F32)
        h2 = r1.astype(BF16)
        h3 = (r1 - h2.astype(F32)).astype(BF16)
        cl_s[...] = _dot(tri_ref[...], jnp.concatenate([h1, h2, h3], axis=0))
        for p in range(N_PAIR):
            lanes = slice(p * PAIR, (p + 1) * PAIR)
            cl = cl_s[:, lanes]
            lwp = lw_s[rows, lanes]
            e_pos = jnp.exp(cl)
            e_neg = jnp.exp(-cl)
            e_ex = jnp.exp(cl - lwp)
            wc = e_pos[CHUNK - 1:CHUNK]
            v = v_s[rows, lanes]
            kt = k_s[rows, lanes] * e_neg
            bt = be_s[rows, lanes] * e_neg
            a2 = stack(-kkn_s[rows, lanes] * e_ex)
            r2 = stack(r_s[rows, lanes] * e_pos)
            b2 = stack(bt)
            k2 = stack(kt)
            v2 = stack(v)
            bh2 = stack(bt * wc)
            kh2 = stack(kt * wc)
            g = _dot_nt(jnp.concatenate([a2, r2], axis=0).astype(BF16),
                        jnp.concatenate([b2, k2], axis=0).astype(BF16))
            lab = jnp.where(strict, g[:PAIR, :PAIR], 0.0)
            lak = jnp.where(strict, g[:PAIR, PAIR:], 0.0)
            grb = jnp.where(incl, g[PAIR:, :PAIR], 0.0)
            grk = jnp.where(incl, g[PAIR:, PAIR:], 0.0)
            tinv = _unit_lower_inverse(lab, masks)
            v2b = v2.astype(BF16)
            lv = _dot(lak.astype(BF16), v2b)
            pm = _dot(tinv.astype(BF16), jnp.concatenate([a2, lv], axis=1).astype(BF16))
            z = z_s[p]
            zb = z.astype(BF16)
            u = _dot(pm[:, :PAIR].astype(BF16), zb) + pm[:, PAIR:]
            ub = u.astype(BF16)
            y = _dot(jnp.concatenate([r2, grb, grk], axis=1).astype(BF16),
                     jnp.concatenate([zb, ub, v2b], axis=0))
            y_s[rows, lanes] = y[:CHUNK] + y[CHUNK:]
            wcol = jnp.transpose(jnp.broadcast_to(wc, (PAIR, PAIR)))
            z_s[p] = wcol * z + _dot(
                jnp.concatenate([jnp.transpose(bh2), jnp.transpose(kh2)], axis=1).astype(BF16),
                jnp.concatenate([ub, v2b], axis=0))
        return carry
    lax.fori_loop(0, tt // CHUNK, chunk_body, 0)

    inv_n = 1.0 / HEAD

    def mean_blk(r0):
        _store_split(q_s, r0, rb, y_s[pl.ds(r0, rb), :])
    _for_blocks(tt, rb, mean_blk)
    _head_sums(q_s, ee_ref, ss_s)

    def center_blk(r0):
        rows = pl.ds(r0, rb)
        d = y_s[rows, :] - ss_s[rows, :] * inv_n
        y_s[rows, :] = d
        _store_split(q_s, r0, rb, d * d)
    _for_blocks(tt, rb, center_blk)
    _head_sums(q_s, ee_ref, ss_s)

    lg = lg_ref[...]
    lb = lb_ref[...]

    def out_blk(r0):
        rows = pl.ds(r0, rb)
        y = y_s[rows, :] * lax.rsqrt(ss_s[rows, :] * inv_n + GN_EPS) * lg + lb
        y = y + bs_s[rows, :] * v_s[rows, :]
        o_ref[rows, :] = (y * _silu(dz_ref[rows, :])).astype(BF16)
    _for_blocks(tt, rb, out_blk)


def _rwkv(z, zwa, nb, nt, params, v_first, vmix):
    tt = TIME_TILE
    has_vmix = vmix is not None
    row = lambda c: pl.BlockSpec((tt, W), lambda b, t: (b * nt + t, c))
    row0 = pl.BlockSpec((tt, W), lambda b, t: (b * nt + t, 0))
    full = lambda a: pl.BlockSpec(a.shape, lambda b, t: (0,) * a.ndim)
    args = [z, z, z, z, zwa]
    specs = [row(7), row(8), row(9), row(10), pl.BlockSpec((tt, LANE), lambda b, t: (b * nt + t, 0))]
    if has_vmix:
        args.append(v_first)
        specs.append(row0)
    args += list(params)
    specs += [full(a) for a in params]
    if has_vmix:
        args += list(vmix)
        specs += [full(a) for a in vmix]
    n_rows = nb * nt * tt
    out_shape = [jax.ShapeDtypeStruct((n_rows, W), BF16)]
    out_specs = [row0]
    if not has_vmix:
        out_shape.append(jax.ShapeDtypeStruct((n_rows, W), F32))
        out_specs.append(row0)
    tile = lambda dt=F32, rows=tt, cols=W: pltpu.VMEM((rows, cols), dt)
    scratch = [tile(rows=tt + SUBLANE), tile(rows=tt + SUBLANE), tile(rows=tt + SUBLANE),
               tile(rows=tt + SUBLANE, cols=LANE), tile(BF16, cols=LANE), tile(cols=2 * W),
               tile(), tile(), tile(), tile(), tile(), tile(), tile(BF16, cols=2 * W), tile(),
               tile(), tile(), tile(rows=CHUNK), pltpu.VMEM((N_PAIR, PAIR, PAIR), F32)]
    if has_vmix:
        scratch.append(tile())
    outs = pl.pallas_call(
        functools.partial(_rwkv_kernel, has_vmix),
        grid=(nb, nt),
        in_specs=specs,
        out_specs=out_specs,
        out_shape=out_shape,
        scratch_shapes=scratch,
        compiler_params=_params(2),
        name="rwkv7_vmix" if has_vmix else "rwkv7",
    )(*args)
    return (outs[0], v_first) if has_vmix else (outs[0], outs[1])


def _merge_kernel(ga_ref, gb_ref, gc_ref, gd_ref, ma_ref, mb_ref, mc_ref, md_ref,
                  wa_ref, wb_ref, wc_ref, wd_ref, o_ref):
    acc = None
    for g_ref, m_ref, w_ref in ((ga_ref, ma_ref, wa_ref), (gb_ref, mb_ref, wb_ref),
                                (gc_ref, mc_ref, wc_ref), (gd_ref, md_ref, wd_ref)):
        term = _sigmoid(m_ref[...]) * _dot(g_ref[...], w_ref[...])
        acc = term if acc is None else acc + term
    o_ref[...] = acc.astype(BF16)


def _merge(gs, z, ws):
    m = gs[0].shape[0]
    tm = min(512, m)
    tn = 1024
    g_spec = pl.BlockSpec((tm, W), lambda i, j: (i, 0))
    gate = lambda br: pl.BlockSpec((tm, tn), lambda i, j: (i, N_STREAM + br * (D_MODEL // tn) + j))
    w_spec = pl.BlockSpec((W, tn), lambda i, j: (0, j))
    return pl.pallas_call(
        _merge_kernel,
        grid=(m // tm, D_MODEL // tn),
        in_specs=[g_spec] * 4 + [gate(br) for br in range(N_BRANCH)] + [w_spec] * 4,
        out_specs=pl.BlockSpec((tm, tn), lambda i, j: (i, j)),
        out_shape=jax.ShapeDtypeStruct((m, D_MODEL), BF16),
        compiler_params=_params(2),
        name="merge",
    )(*gs, z, z, z, z, *ws)


def _resid_kernel(final, x_ref, m_ref, w_ref, fg_ref, o_ref):
    o_ref[...] = x_ref[...] + _dot(m_ref[...], w_ref[...])
    if final:
        fg = fg_ref[...]

        def blk(r0):
            x = o_ref[pl.ds(r0, 32), :]
            ms = jnp.mean(x * x, axis=-1, keepdims=True)
            o_ref[pl.ds(r0, 32), :] = x * lax.rsqrt(ms + NORM_EPS) * fg
        _for_blocks(o_ref.shape[0], 32, blk)


def _resid(x2, merged, w_o, final_g, final):
    m = x2.shape[0]
    tm = min(512, m)
    return pl.pallas_call(
        functools.partial(_resid_kernel, final),
        grid=(m // tm,),
        in_specs=[pl.BlockSpec((tm, D_MODEL), lambda i: (i, 0)),
                  pl.BlockSpec((tm, D_MODEL), lambda i: (i, 0)),
                  pl.BlockSpec((D_MODEL, D_MODEL), lambda i: (0, 0)),
                  pl.BlockSpec((1, D_MODEL), lambda i: (0, 0))],
        out_specs=pl.BlockSpec((tm, D_MODEL), lambda i: (i, 0)),
        out_shape=jax.ShapeDtypeStruct((m, D_MODEL), F32),
        compiler_params=_params(1),
        name="resid_final" if final else "resid",
    )(x2, merged, w_o, final_g)


def _block_diag(blocks):
    n, c, d = blocks.shape
    eye = jnp.eye(n, dtype=blocks.dtype)
    return (eye[:, None, :, None] * blocks[:, :, None, :]).reshape(n * c, n * d)


def _row(v):
    return v.reshape(1, -1).astype(F32)


def kernel(x, norm_g, w_in, conv_a_w, conv_a_b, lru_wr, lru_br, lru_wi, lru_bi, lru_lambda, w_out_a, conv_b_w, conv_b_b, ln_b_g, ln_b_b, w_out_b, pool_w, pool_scale, w_out_c, mu_rkv, mu_wa, w0, w2, a0, a2, k_k, k_a, r_k, lnx_g, lnx_b, v0, v1, v2, w_out_d, w_o, final_g):
    nb, seq, _ = x.shape
    depth = norm_g.shape[0]
    nt = seq // TIME_TILE
    x2 = x.reshape(nb * seq, D_MODEL)
    n_main = N_STREAM * W
    n_wa = 2 * LOW_RANK

    lane = jnp.arange(LANE)
    head_ones = (lane[:, None] // HEAD == lane[None, :] // HEAD).astype(BF16)
    ee = jnp.concatenate([head_ones, head_ones], axis=0)
    tri = (jnp.arange(CHUNK)[:, None] >= jnp.arange(CHUNK)[None, :]).astype(BF16)
    tri3 = jnp.concatenate([tri, tri, tri], axis=1)

    v_first = None
    for l in range(depth):
        w_main = jnp.concatenate([w_in[l, :, :n_main], w_in[l, :, n_main + n_wa:]], axis=1).astype(BF16)
        w_wa = w_in[l, :, n_main:n_main + n_wa].astype(BF16)
        z, zwa = _inproj(x2, _row(norm_g[l]), w_main, w_wa)

        w_gate = jnp.concatenate([_block_diag(lru_wr[l]), _block_diag(lru_wi[l])], axis=1).astype(BF16)
        b_gate = jnp.concatenate([lru_br[l], lru_bi[l]]).reshape(1, -1)
        g_a = _lru(z, nb, nt, conv_a_w[l], _row(conv_a_b[l]), w_gate, b_gate, _row(lru_lambda[l]))
        g_b = _conf(z, nb, nt, conv_b_w[l], _row(conv_b_b[l]), _row(ln_b_g[l]), _row(ln_b_b[l]))
        g_c = _pool(z, nb, nt, pool_w[l].astype(BF16), _row(pool_scale[l]))

        zero = jnp.zeros((LOW_RANK, W), F32)
        w_wa2 = jnp.concatenate([jnp.concatenate([w2[l], zero], axis=1),
                                 jnp.concatenate([zero, a2[l]], axis=1)], axis=0).astype(BF16)
        params = (mu_rkv[l], _row(mu_wa[l]), _row(w0[l]), w_wa2, _row(a0[l]), _row(k_k[l]),
                  _row(k_a[l]), _row(r_k[l]), _row(lnx_g[l]), _row(lnx_b[l]), ee, tri3)
        vmix = None
        if l > 0:
            v1p = jnp.pad(v1[l - 1], ((0, 0), (0, LANE - VMIX_RANK))).astype(BF16)
            v2p = jnp.pad(v2[l - 1], ((0, LANE - VMIX_RANK), (0, 0))).astype(BF16)
            vmix = (_row(v0[l - 1]), v1p, v2p)
        g_d, v_l = _rwkv(z, zwa, nb, nt, params, v_first, vmix)
        if l == 0:
            v_first = v_l

        merged = _merge((g_a, g_b, g_c, g_d), z,
                        tuple(w[l].astype(BF16) for w in (w_out_a, w_out_b, w_out_c, w_out_d)))
        x2 = _resid(x2, merged, w_o[l].astype(BF16), _row(final_g), l == depth - 1)
    return x2.reshape(nb, seq, D_MODEL)
```

```python
import functools

import jax
import jax.numpy as jnp
from jax import lax
from jax.experimental import pallas as pl
from jax.experimental.pallas import tpu as pltpu

F32 = jnp.float32
BF16 = jnp.bfloat16

D_MODEL = 2048
W = D_MODEL // 2
N_STREAM = 11
N_BRANCH = 4
LRU_BLOCKS = 8
LRU_CONV = 4
LRU_C = 8.0
CONF_WIDTH = 31
POOL_WINDOWS = (2, 4, 8, 16)
POOL_GROUP = W // len(POOL_WINDOWS)
HEAD = 64
LOW_RANK = 64
VMIX_RANK = 32
NORM_EPS = 1e-6
LN_EPS = 1e-5
GN_EPS = 64e-5

LANE = 128
SUBLANE = 8
CHUNK = 64
PAIR = 2 * HEAD
N_PAIR = W // PAIR
VMEM_LIMIT = 56 * 1024 * 1024

TIME_TILE = 256
ROW_BLOCK = 16


def _dot(a, b):
    return jnp.dot(a, b, preferred_element_type=F32)


def _dot_nt(a, b):
    return lax.dot_general(a, b, (((1,), (1,)), ((), ())), preferred_element_type=F32)


def _sigmoid(x):
    return 1.0 / (1.0 + jnp.exp(-x))


def _silu(x):
    return x * _sigmoid(x)


def _softplus(x):
    return jnp.maximum(x, 0.0) + jnp.log1p(jnp.exp(-jnp.abs(x)))


def _for_blocks(n_rows, rb, fn):
    def body(i, carry):
        fn(pl.multiple_of(i * rb, rb))
        return carry
    lax.fori_loop(0, n_rows // rb, body, 0)


def _params(n_grid):
    return pltpu.CompilerParams(dimension_semantics=("arbitrary",) * n_grid,
                                vmem_limit_bytes=VMEM_LIMIT)


def _shift_history(t, ext_ref, n_hist, n_rows):
    @pl.when(t == 0)
    def _():
        ext_ref[0:n_hist, :] = jnp.zeros((n_hist, ext_ref.shape[1]), ext_ref.dtype)

    @pl.when(t > 0)
    def _():
        ext_ref[0:n_hist, :] = ext_ref[n_rows:n_rows + n_hist, :]


def _inproj_kernel(x_ref, g_ref, w_ref, wwa_ref, z_ref, zwa_ref, u_ref):
    tm = x_ref.shape[0]

    @pl.when(pl.program_id(1) == 0)
    def _():
        g = g_ref[...]

        def blk(r0):
            x = x_ref[pl.ds(r0, 32), :]
            ms = jnp.mean(x * x, axis=-1, keepdims=True)
            u_ref[pl.ds(r0, 32), :] = (x * lax.rsqrt(ms + NORM_EPS) * g).astype(BF16)
        _for_blocks(tm, 32, blk)
        zwa_ref[...] = _dot(u_ref[...], wwa_ref[...])

    z_ref[...] = _dot(u_ref[...], w_ref[...])


def _inproj(x2, g, w_main, w_wa):
    m = x2.shape[0]
    n = w_main.shape[1]
    tm = min(1024, m)
    tn = 1024
    return pl.pallas_call(
        _inproj_kernel,
        grid=(m // tm, n // tn),
        in_specs=[
            pl.BlockSpec((tm, D_MODEL), lambda i, j: (i, 0)),
            pl.BlockSpec((1, D_MODEL), lambda i, j: (0, 0)),
            pl.BlockSpec((D_MODEL, tn), lambda i, j: (0, j)),
            pl.BlockSpec((D_MODEL, LANE), lambda i, j: (0, 0)),
        ],
        out_specs=[
            pl.BlockSpec((tm, tn), lambda i, j: (i, j)),
            pl.BlockSpec((tm, LANE), lambda i, j: (i, 0)),
        ],
        out_shape=[jax.ShapeDtypeStruct((m, n), F32), jax.ShapeDtypeStruct((m, LANE), F32)],
        scratch_shapes=[pltpu.VMEM((tm, D_MODEL), BF16)],
        compiler_params=_params(2),
        name="inproj",
    )(x2, g, w_main, w_wa)


def _lru_kernel(ax_ref, az_ref, cw_ref, cb_ref, wg_ref, bg_ref, lam_ref, o_ref,
                xe_ref, xc_ref, gt_ref, h_ref):
    t = pl.program_id(1)
    tt = ax_ref.shape[0]
    rb = ROW_BLOCK
    hist = SUBLANE
    _shift_history(t, xe_ref, hist, tt)

    @pl.when(t == 0)
    def _():
        h_ref[...] = jnp.zeros(h_ref.shape, F32)

    xe_ref[hist:hist + tt, :] = ax_ref[...]
    cw = cw_ref[...]
    cb = cb_ref[...]

    def conv_blk(r0):
        win = xe_ref[pl.ds(r0, rb + hist), :]
        acc = cb + cw[LRU_CONV - 1:LRU_CONV] * win[hist:hist + rb]
        for j in range(LRU_CONV - 1):
            off = hist - (LRU_CONV - 1) + j
            acc = acc + cw[j:j + 1] * win[off:off + rb]
        xc_ref[pl.ds(r0, rb), :] = acc
    _for_blocks(tt, rb, conv_blk)

    gt_ref[...] = _dot(xc_ref[...].astype(BF16), wg_ref[...]) + bg_ref[...]

    neg_c_softplus = -LRU_C * _softplus(-lam_ref[...])
    row = lax.broadcasted_iota(jnp.int32, (rb, W), 0) & (SUBLANE - 1)

    def scan_blk(i, h_prev):
        r0 = pl.multiple_of(i * rb, rb)
        xc = xc_ref[pl.ds(r0, rb), :]
        g = gt_ref[pl.ds(r0, rb), :]
        r = _sigmoid(g[:, :W])
        ig = _sigmoid(g[:, W:])
        log_a = neg_c_softplus * r
        a = jnp.exp(log_a)
        u = jnp.sqrt(-jnp.tanh(log_a) * (1.0 + a * a)) * (ig * xc)
        for s in (1, 2, 4):
            keep = row >= s
            a_s = jnp.where(keep, pltpu.roll(a, s, axis=0), 1.0)
            u_s = jnp.where(keep, pltpu.roll(u, s, axis=0), 0.0)
            u = u + a * u_s
            a = a * a_s
        hs = []
        for q in range(rb // SUBLANE):
            sl = slice(q * SUBLANE, (q + 1) * SUBLANE)
            h = u[sl] + a[sl] * h_prev
            h_prev = jnp.broadcast_to(h[SUBLANE - 1:SUBLANE], (SUBLANE, W))
            hs.append(h)
        h = jnp.concatenate(hs, axis=0)
        o_ref[pl.ds(r0, rb), :] = (h * _silu(az_ref[pl.ds(r0, rb), :])).astype(BF16)
        return h_prev

    h_ref[...] = lax.fori_loop(0, tt // rb, scan_blk, h_ref[...])


def _lru(z, nb, nt, cw, cb, wg, bg, lam):
    tt = TIME_TILE
    row = lambda c: pl.BlockSpec((tt, W), lambda b, t: (b * nt + t, c))
    full = lambda a: pl.BlockSpec(a.shape, lambda b, t: (0,) * a.ndim)
    return pl.pallas_call(
        _lru_kernel,
        grid=(nb, nt),
        in_specs=[row(0), row(1), full(cw), full(cb), full(wg), full(bg), full(lam)],
        out_specs=pl.BlockSpec((tt, W), lambda b, t: (b * nt + t, 0)),
        out_shape=jax.ShapeDtypeStruct((nb * nt * tt, W), BF16),
        scratch_shapes=[pltpu.VMEM((tt + SUBLANE, W), F32), pltpu.VMEM((tt, W), F32),
                        pltpu.VMEM((tt, 2 * W), F32), pltpu.VMEM((SUBLANE, W), F32)],
        compiler_params=_params(2),
        name="rglru",
    )(z, z, cw, cb, wg, bg, lam)


CONF_HIST = 32


def _conf_kernel(v_ref, glu_ref, bz_ref, cw_ref, cb_ref, lg_ref, lb_ref, o_ref, he_ref, cv_ref):
    t = pl.program_id(1)
    tt = v_ref.shape[0]
    rb = 2 * ROW_BLOCK
    n_tiles = W // LANE
    for c in range(n_tiles):
        _shift_history(t, he_ref.at[0, c], CONF_HIST, tt)

    def glu_blk(r0):
        h = v_ref[pl.ds(r0, rb), :] * _sigmoid(glu_ref[pl.ds(r0, rb), :])
        for c in range(n_tiles):
            he_ref[0, c, pl.ds(CONF_HIST + r0, rb), :] = h[:, c * LANE:(c + 1) * LANE]
    _for_blocks(tt, rb, glu_blk)
    n_shifted = tt + CONF_HIST - SUBLANE
    for s in range(1, SUBLANE):
        for c in range(n_tiles):
            he_ref[s, c, 0:n_shifted, :] = he_ref[0, c, s:s + n_shifted, :]

    for c in range(n_tiles):
        lanes = slice(c * LANE, (c + 1) * LANE)
        taps = [cw_ref[j * SUBLANE:(j + 1) * SUBLANE, lanes] for j in range(CONF_WIDTH)]
        bias = jnp.broadcast_to(cb_ref[:, lanes], (SUBLANE, LANE))

        def conv_rows(i, carry, c=c, lanes=lanes, taps=taps, bias=bias):
            r = pl.multiple_of(i * SUBLANE, SUBLANE)
            parts = [bias, None, None, None]
            for j in range(CONF_WIDTH):
                off = CONF_HIST - (CONF_WIDTH - 1) + j
                s = off % SUBLANE
                term = taps[j] * he_ref[s, c, pl.ds(r + (off - s), SUBLANE), :]
                parts[j % 4] = term if parts[j % 4] is None else parts[j % 4] + term
            cv_ref[pl.ds(r, SUBLANE), lanes] = (parts[0] + parts[1]) + (parts[2] + parts[3])
            return carry
        lax.fori_loop(0, tt // SUBLANE, conv_rows, 0, unroll=2)

    lg = lg_ref[...]
    lb = lb_ref[...]

    def norm_blk(r0):
        acc = cv_ref[pl.ds(r0, rb), :]
        mu = jnp.mean(acc, axis=-1, keepdims=True)
        d = acc - mu
        var = jnp.mean(d * d, axis=-1, keepdims=True)
        y = d * lax.rsqrt(var + LN_EPS) * lg + lb
        o_ref[pl.ds(r0, rb), :] = (_silu(y) * _silu(bz_ref[pl.ds(r0, rb), :])).astype(BF16)
    _for_blocks(tt, rb, norm_blk)


def _conf(z, nb, nt, cw, cb, lg, lb):
    tt = TIME_TILE
    row = lambda c: pl.BlockSpec((tt, W), lambda b, t: (b * nt + t, c))
    full = lambda a: pl.BlockSpec(a.shape, lambda b, t: (0,) * a.ndim)
    return pl.pallas_call(
        _conf_kernel,
        grid=(nb, nt),
        in_specs=[row(2), row(3), row(4), full(cw), full(cb), full(lg), full(lb)],
        out_specs=pl.BlockSpec((tt, W), lambda b, t: (b * nt + t, 0)),
        out_shape=jax.ShapeDtypeStruct((nb * nt * tt, W), BF16),
        scratch_shapes=[pltpu.VMEM((SUBLANE, W // LANE, tt + CONF_HIST, LANE), F32),
                        pltpu.VMEM((tt, W), F32)],
        compiler_params=_params(2),
        name="conformer",
    )(z, z, z, cw, cb, lg, lb)


POOL_HIST = 16


def _pool_kernel(cx_ref, cz_ref, pw_ref, ps_ref, o_ref, xe_ref, p_ref, y_ref):
    t = pl.program_id(1)
    tt = cx_ref.shape[0]
    rb = ROW_BLOCK
    _shift_history(t, xe_ref, POOL_HIST, tt)
    xe_ref[POOL_HIST:POOL_HIST + tt, :] = cx_ref[...]
    t0 = t * tt

    def pool_blk(r0):
        win = xe_ref[pl.ds(r0, rb + POOL_HIST), :]
        count = (lax.broadcasted_iota(jnp.int32, (rb, POOL_GROUP), 0) + (t0 + r0 + 1)).astype(F32)
        for gi, w in enumerate(POOL_WINDOWS):
            lanes = slice(gi * POOL_GROUP, (gi + 1) * POOL_GROUP)
            wg = win[:, lanes]
            xg = wg[POOL_HIST:POOL_HIST + rb]
            s = xg
            for j in range(1, w):
                s = s + wg[POOL_HIST - j:POOL_HIST - j + rb]
            p_ref[pl.ds(r0, rb), lanes] = (s / jnp.minimum(count, float(w)) - xg).astype(BF16)
    _for_blocks(tt, rb, pool_blk)

    for gi in range(len(POOL_WINDOWS)):
        lanes = slice(gi * POOL_GROUP, (gi + 1) * POOL_GROUP)
        y_ref[:, lanes] = _dot(p_ref[:, lanes], pw_ref[gi])

    ps = ps_ref[...]

    def out_blk(r0):
        o_ref[pl.ds(r0, rb), :] = (
            y_ref[pl.ds(r0, rb), :] * ps * _silu(cz_ref[pl.ds(r0, rb), :])).astype(BF16)
    _for_blocks(tt, rb, out_blk)


def _pool(z, nb, nt, pw, ps):
    tt = TIME_TILE
    row = lambda c: pl.BlockSpec((tt, W), lambda b, t: (b * nt + t, c))
    full = lambda a: pl.BlockSpec(a.shape, lambda b, t: (0,) * a.ndim)
    return pl.pallas_call(
        _pool_kernel,
        grid=(nb, nt),
        in_specs=[row(5), row(6), full(pw), full(ps)],
        out_specs=pl.BlockSpec((tt, W), lambda b, t: (b * nt + t, 0)),
        out_shape=jax.ShapeDtypeStruct((nb * nt * tt, W), BF16),
        scratch_shapes=[pltpu.VMEM((tt + POOL_HIST, W), F32), pltpu.VMEM((tt, W), BF16),
                        pltpu.VMEM((tt, W), F32)],
        compiler_params=_params(2),
        name="pool",
    )(z, z, pw, ps)


def _split2(x):
    hi = x.astype(BF16)
    lo = (x - hi.astype(F32)).astype(BF16)
    return hi, lo


def _store_split(q_ref, r0, rb, x):
    hi, lo = _split2(x)
    for c in range(W // LANE):
        q_ref[pl.ds(r0, rb), 2 * c * LANE:(2 * c + 1) * LANE] = hi[:, c * LANE:(c + 1) * LANE]
        q_ref[pl.ds(r0, rb), (2 * c + 1) * LANE:(2 * c + 2) * LANE] = lo[:, c * LANE:(c + 1) * LANE]


def _head_sums(q_ref, ee_ref, out_ref):
    for c in range(W // LANE):
        out_ref[:, c * LANE:(c + 1) * LANE] = _dot(q_ref[:, 2 * c * LANE:(2 * c + 2) * LANE], ee_ref[...])


def _unit_lower_inverses(lmats, masks):
    eye, diag16, off32, off64 = masks
    ds = [jnp.where(diag16, m, 0.0) for m in lmats]
    ts = [eye + d for d in ds]
    ps = [d.astype(BF16) for d in ds]
    for _ in range(3):
        ps = [_dot(p, p).astype(BF16) for p in ps]
        ts = [t + _dot(t.astype(BF16), p) for t, p in zip(ts, ps)]
    for off in (off32, off64):
        los = [jnp.where(off, m, 0.0).astype(BF16) for m in lmats]
        tbs = [t.astype(BF16) for t in ts]
        xs = [_dot(lo, tb).astype(BF16) for lo, tb in zip(los, tbs)]
        ts = [t + _dot(tb, x) for t, tb, x in zip(ts, tbs, xs)]
    return ts


def _rwkv_kernel(has_vmix, *refs):
    if has_vmix:
        (pr_ref, pk_ref, pv_ref, dz_ref, zwa_ref, vf_ref, mu_ref, muwa_ref, w0_ref, wwa_ref, a0_ref,
         kk_ref, ka_ref, rk_ref, lg_ref, lb_ref, ee_ref, tri_ref, v0_ref, v1_ref, v2_ref,
         o_ref,
         pe_r, pe_k, pe_v, pe_wa, twa_s, wa_s, r_s, k_s, v_s, lw_s, kkn_s, be_s, q_s, ss_s,
         bs_s, y_s, cl_s, z_s, vm_s) = refs
        vout_ref = None
    else:
        (pr_ref, pk_ref, pv_ref, dz_ref, zwa_ref, mu_ref, muwa_ref, w0_ref, wwa_ref, a0_ref,
         kk_ref, ka_ref, rk_ref, lg_ref, lb_ref, ee_ref, tri_ref,
         o_ref, vout_ref,
         pe_r, pe_k, pe_v, pe_wa, twa_s, wa_s, r_s, k_s, v_s, lw_s, kkn_s, be_s, q_s, ss_s,
         bs_s, y_s, cl_s, z_s) = refs
    t = pl.program_id(1)
    tt = pr_ref.shape[0]
    rb = ROW_BLOCK
    hist = SUBLANE

    for ext in (pe_r, pe_k, pe_v, pe_wa):
        _shift_history(t, ext, hist, tt)

    @pl.when(t == 0)
    def _():
        z_s[...] = jnp.zeros(z_s.shape, F32)

    pe_r[hist:hist + tt, :] = pr_ref[...]
    pe_k[hist:hist + tt, :] = pk_ref[...]
    pe_v[hist:hist + tt, :] = pv_ref[...]
    pe_wa[hist:hist + tt, :] = zwa_ref[...]

    muwa = muwa_ref[...]
    lane_wa = lax.broadcasted_iota(jnp.int32, (rb, LANE), 1)

    def wa_blk(r0):
        win = pe_wa[pl.ds(r0, rb + hist), :]
        cur = win[hist:hist + rb]
        prev = win[hist - 1:hist - 1 + rb]
        s = cur + (prev - cur) * muwa
        twa_s[pl.ds(r0, rb), :] = jnp.where(lane_wa < LOW_RANK, jnp.tanh(s), s).astype(BF16)
    _for_blocks(tt, rb, wa_blk)
    wa_s[...] = _dot(twa_s[...], wwa_ref[...])

    k_k = kk_ref[...]

    def shift_blk(r0):
        for ext, idx, dst in ((pe_r, 0, r_s), (pe_k, 1, k_s), (pe_v, 2, v_s)):
            win = ext[pl.ds(r0, rb + hist), :]
            cur = win[hist:hist + rb]
            prev = win[hist - 1:hist - 1 + rb]
            dst[pl.ds(r0, rb), :] = cur + (prev - cur) * mu_ref[idx:idx + 1, :]
        kkr = k_s[pl.ds(r0, rb), :] * k_k
        _store_split(q_s, r0, rb, kkr * kkr)
    _for_blocks(tt, rb, shift_blk)
    _head_sums(q_s, ee_ref, ss_s)

    if has_vmix:
        vm_s[...] = _dot(_dot(v_s[...].astype(BF16), v1_ref[...]).astype(BF16), v2_ref[...])

    w0 = w0_ref[...]
    a0 = a0_ref[...]
    k_a = ka_ref[...]
    r_k = rk_ref[...]

    def prep_blk(r0):
        rows = pl.ds(r0, rb)
        wa = wa_s[rows, :]
        w_log = -_softplus(-(w0 + wa[:, :W])) - 0.5
        lw_s[rows, :] = -jnp.exp(w_log)
        a = _sigmoid(a0 + wa[:, W:])
        k = k_s[rows, :]
        kkn = (k * k_k) * lax.rsqrt(jnp.maximum(ss_s[rows, :], 1e-24))
        kkn_s[rows, :] = kkn
        be_s[rows, :] = kkn * a
        k_eff = k * (1.0 + (a - 1.0) * k_a)
        k_s[rows, :] = k_eff
        v = v_s[rows, :]
        if has_vmix:
            v = v + (vf_ref[rows, :] - v) * _sigmoid(v0_ref[...] + vm_s[rows, :])
            v_s[rows, :] = v
        else:
            vout_ref[rows, :] = v
        _store_split(q_s, r0, rb, r_s[rows, :] * k_eff * r_k)
    _for_blocks(tt, rb, prep_blk)
    _head_sums(q_s, ee_ref, bs_s)

    ri = lax.broadcasted_iota(jnp.int32, (PAIR, PAIR), 0)
    ci = lax.broadcasted_iota(jnp.int32, (PAIR, PAIR), 1)
    strict = ri > ci
    incl = ri >= ci
    eye = jnp.where(ri == ci, 1.0, 0.0).astype(F32)
    masks = (eye, (ri >> 4) == (ci >> 4),
             ((ri >> 5) == (ci >> 5)) & ((ri >> 4) != (ci >> 4)),
             ((ri >> 6) == (ci >> 6)) & ((ri >> 5) != (ci >> 5)))
    lane_c = lax.broadcasted_iota(jnp.int32, (CHUNK, PAIR), 1)
    first_head = lane_c < HEAD

    def stack(x):
        return jnp.concatenate([jnp.where(first_head, x, 0.0), jnp.where(first_head, 0.0, x)], axis=0)

    def chunk_body(c, carry):
        c0 = pl.multiple_of(c * CHUNK, CHUNK)
        rows = pl.ds(c0, CHUNK)
        lw = lw_s[rows, :]
        h1 = lw.astype(BF16)
        r1 = lw - h1.astype(F32)
        h2 = r1.astype(BF16)
        h3 = (r1 - h2.astype(F32)).astype(BF16)
        cl_s[...] = _dot(tri_ref[...], jnp.concatenate([h1, h2, h3], axis=0))
        pairs = range(N_PAIR)
        a2s, r2s, v2bs, bkts, wcols, gs = [], [], [], [], [], []
        for p in pairs:
            lanes = slice(p * PAIR, (p + 1) * PAIR)
            cl = cl_s[:, lanes]
            e_pos = jnp.exp(cl)
            e_neg = jnp.exp(-cl)
            e_ex = jnp.exp(cl - lw_s[rows, lanes])
            wc = e_pos[CHUNK - 1:CHUNK]
            kt = k_s[rows, lanes] * e_neg
            bt = be_s[rows, lanes] * e_neg
            a2 = stack(-kkn_s[rows, lanes] * e_ex)
            r2 = stack(r_s[rows, lanes] * e_pos)
            gs.append(_dot_nt(jnp.concatenate([a2, r2], axis=0).astype(BF16),
                              jnp.concatenate([stack(bt), stack(kt)], axis=0).astype(BF16)))
            a2s.append(a2)
            r2s.append(r2)
            v2bs.append(stack(v_s[rows, lanes]).astype(BF16))
            bkts.append(jnp.concatenate([jnp.transpose(stack(bt * wc)), jnp.transpose(stack(kt * wc))],
                                        axis=1).astype(BF16))
            wcols.append(jnp.transpose(jnp.broadcast_to(wc, (PAIR, PAIR))))
        labs = [jnp.where(strict, g[:PAIR, :PAIR], 0.0) for g in gs]
        laks = [jnp.where(strict, g[:PAIR, PAIR:], 0.0).astype(BF16) for g in gs]
        ylhs = [jnp.concatenate([r2, jnp.where(incl, g[PAIR:, :PAIR], 0.0),
                                 jnp.where(incl, g[PAIR:, PAIR:], 0.0)], axis=1).astype(BF16)
                for r2, g in zip(r2s, gs)]
        lvs = [_dot(lak, v2b) for lak, v2b in zip(laks, v2bs)]
        tinvs = _unit_lower_inverses(labs, masks)
        pms = [_dot(t.astype(BF16), jnp.concatenate([a2, lv], axis=1).astype(BF16))
               for t, a2, lv in zip(tinvs, a2s, lvs)]
        zs = [z_s[p] for p in pairs]
        zbs = [z.astype(BF16) for z in zs]
        ubs = [(_dot(pm[:, :PAIR].astype(BF16), zb) + pm[:, PAIR:]).astype(BF16)
               for pm, zb in zip(pms, zbs)]
        for p in pairs:
            y = _dot(ylhs[p], jnp.concatenate([zbs[p], ubs[p], v2bs[p]], axis=0))
            y_s[rows, p * PAIR:(p + 1) * PAIR] = y[:CHUNK] + y[CHUNK:]
        for p in pairs:
            z_s[p] = wcols[p] * zs[p] + _dot(bkts[p], jnp.concatenate([ubs[p], v2bs[p]], axis=0))
        return carry
    lax.fori_loop(0, tt // CHUNK, chunk_body, 0)

    inv_n = 1.0 / HEAD

    def mean_blk(r0):
        _store_split(q_s, r0, rb, y_s[pl.ds(r0, rb), :])
    _for_blocks(tt, rb, mean_blk)
    _head_sums(q_s, ee_ref, ss_s)

    def center_blk(r0):
        rows = pl.ds(r0, rb)
        d = y_s[rows, :] - ss_s[rows, :] * inv_n
        y_s[rows, :] = d
        _store_split(q_s, r0, rb, d * d)
    _for_blocks(tt, rb, center_blk)
    _head_sums(q_s, ee_ref, ss_s)

    lg = lg_ref[...]
    lb = lb_ref[...]

    def out_blk(r0):
        rows = pl.ds(r0, rb)
        y = y_s[rows, :] * lax.rsqrt(ss_s[rows, :] * inv_n + GN_EPS) * lg + lb
        y = y + bs_s[rows, :] * v_s[rows, :]
        o_ref[rows, :] = (y * _silu(dz_ref[rows, :])).astype(BF16)
    _for_blocks(tt, rb, out_blk)


def _rwkv(z, zwa, nb, nt, params, v_first, vmix):
    tt = TIME_TILE
    has_vmix = vmix is not None
    row = lambda c: pl.BlockSpec((tt, W), lambda b, t: (b * nt + t, c))
    row0 = pl.BlockSpec((tt, W), lambda b, t: (b * nt + t, 0))
    full = lambda a: pl.BlockSpec(a.shape, lambda b, t: (0,) * a.ndim)
    args = [z, z, z, z, zwa]
    specs = [row(7), row(8), row(9), row(10), pl.BlockSpec((tt, LANE), lambda b, t: (b * nt + t, 0))]
    if has_vmix:
        args.append(v_first)
        specs.append(row0)
    args += list(params)
    specs += [full(a) for a in params]
    if has_vmix:
        args += list(vmix)
        specs += [full(a) for a in vmix]
    n_rows = nb * nt * tt
    out_shape = [jax.ShapeDtypeStruct((n_rows, W), BF16)]
    out_specs = [row0]
    if not has_vmix:
        out_shape.append(jax.ShapeDtypeStruct((n_rows, W), F32))
        out_specs.append(row0)
    tile = lambda dt=F32, rows=tt, cols=W: pltpu.VMEM((rows, cols), dt)
    scratch = [tile(rows=tt + SUBLANE), tile(rows=tt + SUBLANE), tile(rows=tt + SUBLANE),
               tile(rows=tt + SUBLANE, cols=LANE), tile(BF16, cols=LANE), tile(cols=2 * W),
               tile(), tile(), tile(), tile(), tile(), tile(), tile(BF16, cols=2 * W), tile(),
               tile(), tile(), tile(rows=CHUNK), pltpu.VMEM((N_PAIR, PAIR, PAIR), F32)]
    if has_vmix:
        scratch.append(tile())
    outs = pl.pallas_call(
        functools.partial(_rwkv_kernel, has_vmix),
        grid=(nb, nt),
        in_specs=specs,
        out_specs=out_specs,
        out_shape=out_shape,
        scratch_shapes=scratch,
        compiler_params=_params(2),
        name="rwkv7_vmix" if has_vmix else "rwkv7",
    )(*args)
    return (outs[0], v_first) if has_vmix else (outs[0], outs[1])


def _merge_kernel(ga_ref, gb_ref, gc_ref, gd_ref, ma_ref, mb_ref, mc_ref, md_ref,
                  wa_ref, wb_ref, wc_ref, wd_ref, o_ref):
    acc = None
    for g_ref, m_ref, w_ref in ((ga_ref, ma_ref, wa_ref), (gb_ref, mb_ref, wb_ref),
                                (gc_ref, mc_ref, wc_ref), (gd_ref, md_ref, wd_ref)):
        term = _sigmoid(m_ref[...]) * _dot(g_ref[...], w_ref[...])
        acc = term if acc is None else acc + term
    o_ref[...] = acc.astype(BF16)


def _merge(gs, z, ws):
    m = gs[0].shape[0]
    tm = min(512, m)
    tn = 1024
    g_spec = pl.BlockSpec((tm, W), lambda i, j: (i, 0))
    gate = lambda br: pl.BlockSpec((tm, tn), lambda i, j: (i, N_STREAM + br * (D_MODEL // tn) + j))
    w_spec = pl.BlockSpec((W, tn), lambda i, j: (0, j))
    return pl.pallas_call(
        _merge_kernel,
        grid=(m // tm, D_MODEL // tn),
        in_specs=[g_spec] * 4 + [gate(br) for br in range(N_BRANCH)] + [w_spec] * 4,
        out_specs=pl.BlockSpec((tm, tn), lambda i, j: (i, j)),
        out_shape=jax.ShapeDtypeStruct((m, D_MODEL), BF16),
        compiler_params=_params(2),
        name="merge",
    )(*gs, z, z, z, z, *ws)


def _resid_kernel(final, x_ref, m_ref, w_ref, fg_ref, o_ref):
    o_ref[...] = x_ref[...] + _dot(m_ref[...], w_ref[...])
    if final:
        fg = fg_ref[...]

        def blk(r0):
            x = o_ref[pl.ds(r0, 32), :]
            ms = jnp.mean(x * x, axis=-1, keepdims=True)
            o_ref[pl.ds(r0, 32), :] = x * lax.rsqrt(ms + NORM_EPS) * fg
        _for_blocks(o_ref.shape[0], 32, blk)


def _resid(x2, merged, w_o, final_g, final):
    m = x2.shape[0]
    tm = min(512, m)
    return pl.pallas_call(
        functools.partial(_resid_kernel, final),
        grid=(m // tm,),
        in_specs=[pl.BlockSpec((tm, D_MODEL), lambda i: (i, 0)),
                  pl.BlockSpec((tm, D_MODEL), lambda i: (i, 0)),
                  pl.BlockSpec((D_MODEL, D_MODEL), lambda i: (0, 0)),
                  pl.BlockSpec((1, D_MODEL), lambda i: (0, 0))],
        out_specs=pl.BlockSpec((tm, D_MODEL), lambda i: (i, 0)),
        out_shape=jax.ShapeDtypeStruct((m, D_MODEL), F32),
        compiler_params=_params(1),
        name="resid_final" if final else "resid",
    )(x2, merged, w_o, final_g)


def _block_diag(blocks):
    n, c, d = blocks.shape
    eye = jnp.eye(n, dtype=blocks.dtype)
    return (eye[:, None, :, None] * blocks[:, :, None, :]).reshape(n * c, n * d)


def _row(v):
    return v.reshape(1, -1).astype(F32)


def kernel(x, norm_g, w_in, conv_a_w, conv_a_b, lru_wr, lru_br, lru_wi, lru_bi, lru_lambda, w_out_a, conv_b_w, conv_b_b, ln_b_g, ln_b_b, w_out_b, pool_w, pool_scale, w_out_c, mu_rkv, mu_wa, w0, w2, a0, a2, k_k, k_a, r_k, lnx_g, lnx_b, v0, v1, v2, w_out_d, w_o, final_g):
    nb, seq, _ = x.shape
    depth = norm_g.shape[0]
    nt = seq // TIME_TILE
    x2 = x.reshape(nb * seq, D_MODEL)
    n_main = N_STREAM * W
    n_wa = 2 * LOW_RANK

    lane = jnp.arange(LANE)
    head_ones = (lane[:, None] // HEAD == lane[None, :] // HEAD).astype(BF16)
    ee = jnp.concatenate([head_ones, head_ones], axis=0)
    tri = (jnp.arange(CHUNK)[:, None] >= jnp.arange(CHUNK)[None, :]).astype(BF16)
    tri3 = jnp.concatenate([tri, tri, tri], axis=1)

    v_first = None
    for l in range(depth):
        w_main = jnp.concatenate([w_in[l, :, :n_main], w_in[l, :, n_main + n_wa:]], axis=1).astype(BF16)
        w_wa = w_in[l, :, n_main:n_main + n_wa].astype(BF16)
        z, zwa = _inproj(x2, _row(norm_g[l]), w_main, w_wa)

        w_gate = jnp.concatenate([_block_diag(lru_wr[l]), _block_diag(lru_wi[l])], axis=1).astype(BF16)
        b_gate = jnp.concatenate([lru_br[l], lru_bi[l]]).reshape(1, -1)
        g_a = _lru(z, nb, nt, conv_a_w[l], _row(conv_a_b[l]), w_gate, b_gate, _row(lru_lambda[l]))
        g_b = _conf(z, nb, nt, jnp.repeat(conv_b_w[l], SUBLANE, axis=0), _row(conv_b_b[l]),
                    _row(ln_b_g[l]), _row(ln_b_b[l]))
        g_c = _pool(z, nb, nt, pool_w[l].astype(BF16), _row(pool_scale[l]))

        zero = jnp.zeros((LOW_RANK, W), F32)
        w_wa2 = jnp.concatenate([jnp.concatenate([w2[l], zero], axis=1),
                                 jnp.concatenate([zero, a2[l]], axis=1)], axis=0).astype(BF16)
        params = (mu_rkv[l], _row(mu_wa[l]), _row(w0[l]), w_wa2, _row(a0[l]), _row(k_k[l]),
                  _row(k_a[l]), _row(r_k[l]), _row(lnx_g[l]), _row(lnx_b[l]), ee, tri3)
        vmix = None
        if l > 0:
            v1p = jnp.pad(v1[l - 1], ((0, 0), (0, LANE - VMIX_RANK))).astype(BF16)
            v2p = jnp.pad(v2[l - 1], ((0, LANE - VMIX_RANK), (0, 0))).astype(BF16)
            vmix = (_row(v0[l - 1]), v1p, v2p)
        g_d, v_l = _rwkv(z, zwa, nb, nt, params, v_first, vmix)
        if l == 0:
            v_first = v_l

        merged = _merge((g_a, g_b, g_c, g_d), z,
                        tuple(w[l].astype(BF16) for w in (w_out_a, w_out_b, w_out_c, w_out_d)))
        x2 = _resid(x2, merged, w_o[l].astype(BF16), _row(final_g), l == depth - 1)
    return x2.reshape(nb, seq, D_MODEL)
```

```python
import functools

import jax
import jax.numpy as jnp
from jax import lax
from jax.experimental import pallas as pl
from jax.experimental.pallas import tpu as pltpu

F32 = jnp.float32
BF16 = jnp.bfloat16

D_MODEL = 2048
W = D_MODEL // 2
N_STREAM = 11
N_BRANCH = 4
LRU_BLOCKS = 8
LRU_CONV = 4
LRU_C = 8.0
CONF_WIDTH = 31
POOL_WINDOWS = (2, 4, 8, 16)
POOL_GROUP = W // len(POOL_WINDOWS)
HEAD = 64
LOW_RANK = 64
VMIX_RANK = 32
NORM_EPS = 1e-6
LN_EPS = 1e-5
GN_EPS = 64e-5

LANE = 128
SUBLANE = 8
CHUNK = 64
CHUNK_GROUP = 2
PAIR = 2 * HEAD
N_PAIR = W // PAIR
VMEM_LIMIT = 56 * 1024 * 1024

TIME_TILE = 256
ROW_BLOCK = 16


def _dot(a, b):
    return jnp.dot(a, b, preferred_element_type=F32)


def _dot_nt(a, b):
    return lax.dot_general(a, b, (((1,), (1,)), ((), ())), preferred_element_type=F32)


def _sigmoid(x):
    return 1.0 / (1.0 + jnp.exp(-x))


def _silu(x):
    return x * _sigmoid(x)


def _softplus(x):
    return jnp.maximum(x, 0.0) + jnp.log1p(jnp.exp(-jnp.abs(x)))


def _for_blocks(n_rows, rb, fn, unroll=1):
    def body(i, carry):
        fn(pl.multiple_of(i * rb, rb))
        return carry
    lax.fori_loop(0, n_rows // rb, body, 0, unroll=unroll)


def _params(n_grid):
    return pltpu.CompilerParams(dimension_semantics=("arbitrary",) * n_grid,
                                vmem_limit_bytes=VMEM_LIMIT)


def _shift_history(t, ext_ref, n_hist, n_rows):
    @pl.when(t == 0)
    def _():
        ext_ref[0:n_hist, :] = jnp.zeros((n_hist, ext_ref.shape[1]), ext_ref.dtype)

    @pl.when(t > 0)
    def _():
        ext_ref[0:n_hist, :] = ext_ref[n_rows:n_rows + n_hist, :]


def _inproj_kernel(x_ref, g_ref, w_ref, wwa_ref, z_ref, zwa_ref, u_ref):
    tm = x_ref.shape[0]

    @pl.when(pl.program_id(1) == 0)
    def _():
        g = g_ref[...]

        def blk(r0):
            x = x_ref[pl.ds(r0, 32), :]
            ms = jnp.mean(x * x, axis=-1, keepdims=True)
            u_ref[pl.ds(r0, 32), :] = (x * lax.rsqrt(ms + NORM_EPS) * g).astype(BF16)
        _for_blocks(tm, 32, blk)
        zwa_ref[...] = _dot(u_ref[...], wwa_ref[0])

    z_ref[...] = _dot(u_ref[...], w_ref[0])


def _inproj(x2, g, w_in, layer):
    m = x2.shape[0]
    n_main = N_STREAM * W
    n_wa = 2 * LOW_RANK
    n = w_in.shape[2] - n_wa
    tm = min(1024, m)
    tn = 1024

    def w_window(i, j):
        col = j * tn + jnp.where(j >= N_STREAM, n_wa, 0)
        return (layer, 0, pl.multiple_of(col, LANE))

    return pl.pallas_call(
        _inproj_kernel,
        grid=(m // tm, n // tn),
        in_specs=[
            pl.BlockSpec((tm, D_MODEL), lambda i, j: (i, 0)),
            pl.BlockSpec((1, D_MODEL), lambda i, j: (0, 0)),
            pl.BlockSpec((pl.Element(1), pl.Element(D_MODEL), pl.Element(tn)), w_window),
            pl.BlockSpec((pl.Element(1), pl.Element(D_MODEL), pl.Element(LANE)),
                         lambda i, j: (layer, 0, n_main)),
        ],
        out_specs=[
            pl.BlockSpec((tm, tn), lambda i, j: (i, j)),
            pl.BlockSpec((tm, LANE), lambda i, j: (i, 0)),
        ],
        out_shape=[jax.ShapeDtypeStruct((m, n), F32), jax.ShapeDtypeStruct((m, LANE), F32)],
        scratch_shapes=[pltpu.VMEM((tm, D_MODEL), BF16)],
        compiler_params=_params(2),
        name="inproj",
    )(x2, g, w_in, w_in)


def _lru_kernel(ax_ref, az_ref, cw_ref, cb_ref, wg_ref, bg_ref, lam_ref, o_ref,
                xe_ref, xc_ref, gt_ref, h_ref):
    t = pl.program_id(1)
    tt = ax_ref.shape[0]
    rb = ROW_BLOCK
    hist = SUBLANE
    _shift_history(t, xe_ref, hist, tt)

    @pl.when(t == 0)
    def _():
        h_ref[...] = jnp.zeros(h_ref.shape, F32)

    xe_ref[hist:hist + tt, :] = ax_ref[...]
    cw = cw_ref[...]
    cb = cb_ref[...]

    def conv_blk(r0):
        win = xe_ref[pl.ds(r0, rb + hist), :]
        acc = cb + cw[LRU_CONV - 1:LRU_CONV] * win[hist:hist + rb]
        for j in range(LRU_CONV - 1):
            off = hist - (LRU_CONV - 1) + j
            acc = acc + cw[j:j + 1] * win[off:off + rb]
        xc_ref[pl.ds(r0, rb), :] = acc
    _for_blocks(tt, rb, conv_blk)

    pw = wg_ref.shape[1]
    for q in range(W // pw):
        cols = slice(q * pw, (q + 1) * pw)
        icols = slice(W + q * pw, W + (q + 1) * pw)
        out = _dot(xc_ref[:, cols].astype(BF16), wg_ref[q])
        gt_ref[:, cols] = out[:, :pw] + bg_ref[:, cols]
        gt_ref[:, icols] = out[:, pw:] + bg_ref[:, icols]

    neg_c_softplus = -LRU_C * _softplus(-lam_ref[...])
    row = lax.broadcasted_iota(jnp.int32, (rb, W), 0) & (SUBLANE - 1)

    def scan_blk(i, h_prev):
        r0 = pl.multiple_of(i * rb, rb)
        xc = xc_ref[pl.ds(r0, rb), :]
        g = gt_ref[pl.ds(r0, rb), :]
        r = _sigmoid(g[:, :W])
        ig = _sigmoid(g[:, W:])
        log_a = neg_c_softplus * r
        a = jnp.exp(log_a)
        u = jnp.sqrt(-jnp.tanh(log_a) * (1.0 + a * a)) * (ig * xc)
        for s in (1, 2, 4):
            keep = row >= s
            a_s = jnp.where(keep, pltpu.roll(a, s, axis=0), 1.0)
            u_s = jnp.where(keep, pltpu.roll(u, s, axis=0), 0.0)
            u = u + a * u_s
            a = a * a_s
        hs = []
        for q in range(rb // SUBLANE):
            sl = slice(q * SUBLANE, (q + 1) * SUBLANE)
            h = u[sl] + a[sl] * h_prev
            h_prev = jnp.broadcast_to(h[SUBLANE - 1:SUBLANE], (SUBLANE, W))
            hs.append(h)
        h = jnp.concatenate(hs, axis=0)
        o_ref[pl.ds(r0, rb), :] = (h * _silu(az_ref[pl.ds(r0, rb), :])).astype(BF16)
        return h_prev

    h_ref[...] = lax.fori_loop(0, tt // rb, scan_blk, h_ref[...])


def _lru(z, nb, nt, cw, cb, wg, bg, lam):
    tt = TIME_TILE
    row = lambda c: pl.BlockSpec((tt, W), lambda b, t: (b * nt + t, c))
    full = lambda a: pl.BlockSpec(a.shape, lambda b, t: (0,) * a.ndim)
    return pl.pallas_call(
        _lru_kernel,
        grid=(nb, nt),
        in_specs=[row(0), row(1), full(cw), full(cb), full(wg), full(bg), full(lam)],
        out_specs=pl.BlockSpec((tt, W), lambda b, t: (b * nt + t, 0)),
        out_shape=jax.ShapeDtypeStruct((nb * nt * tt, W), BF16),
        scratch_shapes=[pltpu.VMEM((tt + SUBLANE, W), F32), pltpu.VMEM((tt, W), F32),
                        pltpu.VMEM((tt, 2 * W), F32), pltpu.VMEM((SUBLANE, W), F32)],
        compiler_params=_params(2),
        name="rglru",
    )(z, z, cw, cb, wg, bg, lam)


CONF_HIST = 32


def _conf_kernel(v_ref, glu_ref, bz_ref, cw_ref, cb_ref, lg_ref, lb_ref, o_ref, he_ref, cv_ref):
    t = pl.program_id(1)
    tt = v_ref.shape[0]
    rb = 2 * ROW_BLOCK
    n_tiles = W // LANE
    for c in range(n_tiles):
        _shift_history(t, he_ref.at[0, c], CONF_HIST, tt)

    def glu_blk(r0):
        h = v_ref[pl.ds(r0, rb), :] * _sigmoid(glu_ref[pl.ds(r0, rb), :])
        for c in range(n_tiles):
            he_ref[0, c, pl.ds(CONF_HIST + r0, rb), :] = h[:, c * LANE:(c + 1) * LANE]
    _for_blocks(tt, rb, glu_blk)
    n_shifted = tt + CONF_HIST - SUBLANE
    for s in range(1, SUBLANE):
        for c in range(n_tiles):
            he_ref[s, c, 0:n_shifted, :] = he_ref[0, c, s:s + n_shifted, :]

    for c in range(n_tiles):
        lanes = slice(c * LANE, (c + 1) * LANE)
        taps = [cw_ref[j * SUBLANE:(j + 1) * SUBLANE, lanes] for j in range(CONF_WIDTH)]
        bias = jnp.broadcast_to(cb_ref[:, lanes], (SUBLANE, LANE))

        def conv_rows(i, carry, c=c, lanes=lanes, taps=taps, bias=bias):
            r = pl.multiple_of(i * SUBLANE, SUBLANE)
            parts = [bias, None, None, None]
            for j in range(CONF_WIDTH):
                off = CONF_HIST - (CONF_WIDTH - 1) + j
                s = off % SUBLANE
                term = taps[j] * he_ref[s, c, pl.ds(r + (off - s), SUBLANE), :]
                parts[j % 4] = term if parts[j % 4] is None else parts[j % 4] + term
            cv_ref[pl.ds(r, SUBLANE), lanes] = (parts[0] + parts[1]) + (parts[2] + parts[3])
            return carry
        lax.fori_loop(0, tt // SUBLANE, conv_rows, 0, unroll=2)

    lg = lg_ref[...]
    lb = lb_ref[...]

    def norm_blk(r0):
        acc = cv_ref[pl.ds(r0, rb), :]
        mu = jnp.mean(acc, axis=-1, keepdims=True)
        d = acc - mu
        var = jnp.mean(d * d, axis=-1, keepdims=True)
        y = d * lax.rsqrt(var + LN_EPS) * lg + lb
        o_ref[pl.ds(r0, rb), :] = (_silu(y) * _silu(bz_ref[pl.ds(r0, rb), :])).astype(BF16)
    _for_blocks(tt, rb, norm_blk, unroll=2)


def _conf(z, nb, nt, cw, cb, lg, lb):
    tt = TIME_TILE
    row = lambda c: pl.BlockSpec((tt, W), lambda b, t: (b * nt + t, c))
    full = lambda a: pl.BlockSpec(a.shape, lambda b, t: (0,) * a.ndim)
    return pl.pallas_call(
        _conf_kernel,
        grid=(nb, nt),
        in_specs=[row(2), row(3), row(4), full(cw), full(cb), full(lg), full(lb)],
        out_specs=pl.BlockSpec((tt, W), lambda b, t: (b * nt + t, 0)),
        out_shape=jax.ShapeDtypeStruct((nb * nt * tt, W), BF16),
        scratch_shapes=[pltpu.VMEM((SUBLANE, W // LANE, tt + CONF_HIST, LANE), F32),
                        pltpu.VMEM((tt, W), F32)],
        compiler_params=_params(2),
        name="conformer",
    )(z, z, z, cw, cb, lg, lb)


POOL_HIST = 16


def _pool_kernel(cx_ref, cz_ref, pw_ref, ps_ref, o_ref, xe_ref, p_ref, y_ref):
    t = pl.program_id(1)
    tt = cx_ref.shape[0]
    rb = ROW_BLOCK
    _shift_history(t, xe_ref, POOL_HIST, tt)
    xe_ref[POOL_HIST:POOL_HIST + tt, :] = cx_ref[...]
    t0 = t * tt

    def pool_blk(r0):
        win = xe_ref[pl.ds(r0, rb + POOL_HIST), :]
        count = (lax.broadcasted_iota(jnp.int32, (rb, POOL_GROUP), 0) + (t0 + r0 + 1)).astype(F32)
        for gi, w in enumerate(POOL_WINDOWS):
            lanes = slice(gi * POOL_GROUP, (gi + 1) * POOL_GROUP)
            wg = win[:, lanes]
            xg = wg[POOL_HIST:POOL_HIST + rb]
            s = xg
            for j in range(1, w):
                s = s + wg[POOL_HIST - j:POOL_HIST - j + rb]
            p_ref[pl.ds(r0, rb), lanes] = (s / jnp.minimum(count, float(w)) - xg).astype(BF16)
    _for_blocks(tt, rb, pool_blk)

    for gi in range(len(POOL_WINDOWS)):
        lanes = slice(gi * POOL_GROUP, (gi + 1) * POOL_GROUP)
        y_ref[:, lanes] = _dot(p_ref[:, lanes], pw_ref[gi])

    ps = ps_ref[...]

    def out_blk(r0):
        o_ref[pl.ds(r0, rb), :] = (
            y_ref[pl.ds(r0, rb), :] * ps * _silu(cz_ref[pl.ds(r0, rb), :])).astype(BF16)
    _for_blocks(tt, rb, out_blk)


def _pool(z, nb, nt, pw, ps):
    tt = TIME_TILE
    row = lambda c: pl.BlockSpec((tt, W), lambda b, t: (b * nt + t, c))
    full = lambda a: pl.BlockSpec(a.shape, lambda b, t: (0,) * a.ndim)
    return pl.pallas_call(
        _pool_kernel,
        grid=(nb, nt),
        in_specs=[row(5), row(6), full(pw), full(ps)],
        out_specs=pl.BlockSpec((tt, W), lambda b, t: (b * nt + t, 0)),
        out_shape=jax.ShapeDtypeStruct((nb * nt * tt, W), BF16),
        scratch_shapes=[pltpu.VMEM((tt + POOL_HIST, W), F32), pltpu.VMEM((tt, W), BF16),
                        pltpu.VMEM((tt, W), F32)],
        compiler_params=_params(2),
        name="pool",
    )(z, z, pw, ps)


def _split2(x):
    hi = x.astype(BF16)
    lo = (x - hi.astype(F32)).astype(BF16)
    return hi, lo


def _store_split(q_ref, r0, rb, x):
    hi, lo = _split2(x)
    for c in range(W // LANE):
        q_ref[pl.ds(r0, rb), 2 * c * LANE:(2 * c + 1) * LANE] = hi[:, c * LANE:(c + 1) * LANE]
        q_ref[pl.ds(r0, rb), (2 * c + 1) * LANE:(2 * c + 2) * LANE] = lo[:, c * LANE:(c + 1) * LANE]


def _head_sums(q_ref, ee_ref, out_ref):
    for c in range(W // LANE):
        out_ref[:, c * LANE:(c + 1) * LANE] = _dot(q_ref[:, 2 * c * LANE:(2 * c + 2) * LANE], ee_ref[...])


def _unit_lower_inverses(lmats, masks):
    eye, diag16, off32, off64 = masks
    n = lmats[0].shape[0]
    ds = [jnp.where(diag16, m, 0.0) for m in lmats]
    ts = [eye + d for d in ds]
    ps = [d.astype(BF16) for d in ds]
    ps = [_dot(p, p).astype(BF16) for p in ps]
    for _ in range(2):
        both = [_dot(jnp.concatenate([p, t.astype(BF16)], axis=0), p) for p, t in zip(ps, ts)]
        ts = [t + b[n:] for t, b in zip(ts, both)]
        ps = [b[:n].astype(BF16) for b in both]
    ts = [t + _dot(t.astype(BF16), p) for t, p in zip(ts, ps)]
    for off in (off32, off64):
        los = [jnp.where(off, m, 0.0).astype(BF16) for m in lmats]
        tbs = [t.astype(BF16) for t in ts]
        xs = [_dot(lo, tb).astype(BF16) for lo, tb in zip(los, tbs)]
        ts = [t + _dot(tb, x) for t, tb, x in zip(ts, tbs, xs)]
    return ts


def _rwkv_kernel(has_vmix, *refs):
    if has_vmix:
        (pr_ref, pk_ref, pv_ref, dz_ref, zwa_ref, vf_ref, mu_ref, muwa_ref, w0_ref, wwa_ref, a0_ref,
         kk_ref, ka_ref, rk_ref, lg_ref, lb_ref, ee_ref, tri_ref, v0_ref, v1_ref, v2_ref,
         o_ref,
         pe_r, pe_k, pe_v, pe_wa, twa_s, wa_s, r_s, k_s, v_s, lw_s, kkn_s, be_s, q_s, ss_s,
         bs_s, y_s, cl_s, z_s, vm_s) = refs
        vout_ref = None
    else:
        (pr_ref, pk_ref, pv_ref, dz_ref, zwa_ref, mu_ref, muwa_ref, w0_ref, wwa_ref, a0_ref,
         kk_ref, ka_ref, rk_ref, lg_ref, lb_ref, ee_ref, tri_ref,
         o_ref, vout_ref,
         pe_r, pe_k, pe_v, pe_wa, twa_s, wa_s, r_s, k_s, v_s, lw_s, kkn_s, be_s, q_s, ss_s,
         bs_s, y_s, cl_s, z_s) = refs
    t = pl.program_id(1)
    tt = pr_ref.shape[0]
    rb = ROW_BLOCK
    hist = SUBLANE

    for ext in (pe_r, pe_k, pe_v, pe_wa):
        _shift_history(t, ext, hist, tt)

    @pl.when(t == 0)
    def _():
        z_s[...] = jnp.zeros(z_s.shape, F32)

    pe_r[hist:hist + tt, :] = pr_ref[...]
    pe_k[hist:hist + tt, :] = pk_ref[...]
    pe_v[hist:hist + tt, :] = pv_ref[...]
    pe_wa[hist:hist + tt, :] = zwa_ref[...]

    muwa = muwa_ref[...]
    lane_wa = lax.broadcasted_iota(jnp.int32, (rb, LANE), 1)

    def wa_blk(r0):
        win = pe_wa[pl.ds(r0, rb + hist), :]
        cur = win[hist:hist + rb]
        prev = win[hist - 1:hist - 1 + rb]
        s = cur + (prev - cur) * muwa
        twa_s[pl.ds(r0, rb), :] = jnp.where(lane_wa < LOW_RANK, jnp.tanh(s), s).astype(BF16)
    _for_blocks(tt, rb, wa_blk)
    wa_s[...] = _dot(twa_s[...], wwa_ref[...])

    k_k = kk_ref[...]

    def shift_blk(r0):
        for ext, idx, dst in ((pe_r, 0, r_s), (pe_k, 1, k_s), (pe_v, 2, v_s)):
            win = ext[pl.ds(r0, rb + hist), :]
            cur = win[hist:hist + rb]
            prev = win[hist - 1:hist - 1 + rb]
            dst[pl.ds(r0, rb), :] = cur + (prev - cur) * mu_ref[idx:idx + 1, :]
        kkr = k_s[pl.ds(r0, rb), :] * k_k
        _store_split(q_s, r0, rb, kkr * kkr)
    _for_blocks(tt, rb, shift_blk)
    _head_sums(q_s, ee_ref, ss_s)

    if has_vmix:
        vm_s[...] = _dot(_dot(v_s[...].astype(BF16), v1_ref[...]).astype(BF16), v2_ref[...])

    w0 = w0_ref[...]
    a0 = a0_ref[...]
    k_a = ka_ref[...]
    r_k = rk_ref[...]

    def prep_blk(r0):
        rows = pl.ds(r0, rb)
        wa = wa_s[rows, :]
        w_log = -_softplus(-(w0 + wa[:, :W])) - 0.5
        lw_s[rows, :] = -jnp.exp(w_log)
        a = _sigmoid(a0 + wa[:, W:])
        k = k_s[rows, :]
        kkn = (k * k_k) * lax.rsqrt(jnp.maximum(ss_s[rows, :], 1e-24))
        kkn_s[rows, :] = kkn
        be_s[rows, :] = kkn * a
        k_eff = k * (1.0 + (a - 1.0) * k_a)
        k_s[rows, :] = k_eff
        v = v_s[rows, :]
        if has_vmix:
            v = v + (vf_ref[rows, :] - v) * _sigmoid(v0_ref[...] + vm_s[rows, :])
            v_s[rows, :] = v
        else:
            vout_ref[rows, :] = v
        _store_split(q_s, r0, rb, r_s[rows, :] * k_eff * r_k)
    _for_blocks(tt, rb, prep_blk)
    _head_sums(q_s, ee_ref, bs_s)

    ri = lax.broadcasted_iota(jnp.int32, (PAIR, PAIR), 0)
    ci = lax.broadcasted_iota(jnp.int32, (PAIR, PAIR), 1)
    strict = ri > ci
    incl = ri >= ci
    eye = jnp.where(ri == ci, 1.0, 0.0).astype(F32)
    masks = (eye, (ri >> 4) == (ci >> 4),
             ((ri >> 5) == (ci >> 5)) & ((ri >> 4) != (ci >> 4)),
             ((ri >> 6) == (ci >> 6)) & ((ri >> 5) != (ci >> 5)))
    lane_c = lax.broadcasted_iota(jnp.int32, (CHUNK, PAIR), 1)
    first_head = lane_c < HEAD

    def stack(x):
        return jnp.concatenate([jnp.where(first_head, x, 0.0), jnp.where(first_head, 0.0, x)], axis=0)

    def chunk_body(c, carry):
        a2s, r2s, v2bs, bkts, wcols, gs = [], [], [], [], [], []
        for q in range(CHUNK_GROUP):
            rows = pl.ds(pl.multiple_of((c * CHUNK_GROUP + q) * CHUNK, CHUNK), CHUNK)
            crow = slice(q * CHUNK, (q + 1) * CHUNK)
            lw = lw_s[rows, :]
            h1 = lw.astype(BF16)
            r1 = lw - h1.astype(F32)
            h2 = r1.astype(BF16)
            h3 = (r1 - h2.astype(F32)).astype(BF16)
            cl_s[crow, :] = _dot(tri_ref[...], jnp.concatenate([h1, h2, h3], axis=0))
            for p in range(N_PAIR):
                lanes = slice(p * PAIR, (p + 1) * PAIR)
                cl = cl_s[crow, lanes]
                e_pos = jnp.exp(cl)
                e_neg = jnp.exp(-cl)
                e_ex = jnp.exp(cl - lw_s[rows, lanes])
                wc = e_pos[CHUNK - 1:CHUNK]
                kt = k_s[rows, lanes] * e_neg
                bt = be_s[rows, lanes] * e_neg
                a2 = stack(-kkn_s[rows, lanes] * e_ex)
                r2 = stack(r_s[rows, lanes] * e_pos)
                gs.append(_dot_nt(jnp.concatenate([a2, r2], axis=0).astype(BF16),
                                  jnp.concatenate([stack(bt), stack(kt)], axis=0).astype(BF16)))
                a2s.append(a2)
                r2s.append(r2)
                v2bs.append(stack(v_s[rows, lanes]).astype(BF16))
                bkts.append(jnp.concatenate([jnp.transpose(stack(bt * wc)),
                                             jnp.transpose(stack(kt * wc))], axis=1).astype(BF16))
                wcols.append(jnp.transpose(jnp.broadcast_to(wc, (PAIR, PAIR))))
        labs = [jnp.where(strict, g[:PAIR, :PAIR], 0.0) for g in gs]
        laks = [jnp.where(strict, g[:PAIR, PAIR:], 0.0).astype(BF16) for g in gs]
        ylhs = [jnp.concatenate([r2, jnp.where(incl, g[PAIR:, :PAIR], 0.0),
                                 jnp.where(incl, g[PAIR:, PAIR:], 0.0)], axis=1).astype(BF16)
                for r2, g in zip(r2s, gs)]
        lvs = [_dot(lak, v2b) for lak, v2b in zip(laks, v2bs)]
        tinvs = _unit_lower_inverses(labs, masks)
        pms = [_dot(t.astype(BF16), jnp.concatenate([a2, lv], axis=1).astype(BF16))
               for t, a2, lv in zip(tinvs, a2s, lvs)]
        for q in range(CHUNK_GROUP):
            rows = pl.ds(pl.multiple_of((c * CHUNK_GROUP + q) * CHUNK, CHUNK), CHUNK)
            ids = [q * N_PAIR + p for p in range(N_PAIR)]
            zs = [z_s[p] for p in range(N_PAIR)]
            zbs = [z.astype(BF16) for z in zs]
            ubs = [(_dot(pms[i][:, :PAIR].astype(BF16), zb) + pms[i][:, PAIR:]).astype(BF16)
                   for i, zb in zip(ids, zbs)]
            for p, i in enumerate(ids):
                y = _dot(ylhs[i], jnp.concatenate([zbs[p], ubs[p], v2bs[i]], axis=0))
                y_s[rows, p * PAIR:(p + 1) * PAIR] = y[:CHUNK] + y[CHUNK:]
            for p, i in enumerate(ids):
                z_s[p] = wcols[i] * zs[p] + _dot(bkts[i], jnp.concatenate([ubs[p], v2bs[i]], axis=0))
        return carry
    lax.fori_loop(0, tt // (CHUNK * CHUNK_GROUP), chunk_body, 0)

    inv_n = 1.0 / HEAD

    def mean_blk(r0):
        _store_split(q_s, r0, rb, y_s[pl.ds(r0, rb), :])
    _for_blocks(tt, rb, mean_blk)
    _head_sums(q_s, ee_ref, ss_s)

    def center_blk(r0):
        rows = pl.ds(r0, rb)
        d = y_s[rows, :] - ss_s[rows, :] * inv_n
        y_s[rows, :] = d
        _store_split(q_s, r0, rb, d * d)
    _for_blocks(tt, rb, center_blk)
    _head_sums(q_s, ee_ref, ss_s)

    lg = lg_ref[...]
    lb = lb_ref[...]

    def out_blk(r0):
        rows = pl.ds(r0, rb)
        y = y_s[rows, :] * lax.rsqrt(ss_s[rows, :] * inv_n + GN_EPS) * lg + lb
        y = y + bs_s[rows, :] * v_s[rows, :]
        o_ref[rows, :] = (y * _silu(dz_ref[rows, :])).astype(BF16)
    _for_blocks(tt, rb, out_blk)


def _rwkv(z, zwa, nb, nt, params, v_first, vmix):
    tt = TIME_TILE
    has_vmix = vmix is not None
    row = lambda c: pl.BlockSpec((tt, W), lambda b, t: (b * nt + t, c))
    row0 = pl.BlockSpec((tt, W), lambda b, t: (b * nt + t, 0))
    full = lambda a: pl.BlockSpec(a.shape, lambda b, t: (0,) * a.ndim)
    args = [z, z, z, z, zwa]
    specs = [row(7), row(8), row(9), row(10), pl.BlockSpec((tt, LANE), lambda b, t: (b * nt + t, 0))]
    if has_vmix:
        args.append(v_first)
        specs.append(row0)
    args += list(params)
    specs += [full(a) for a in params]
    if has_vmix:
        args += list(vmix)
        specs += [full(a) for a in vmix]
    n_rows = nb * nt * tt
    out_shape = [jax.ShapeDtypeStruct((n_rows, W), BF16)]
    out_specs = [row0]
    if not has_vmix:
        out_shape.append(jax.ShapeDtypeStruct((n_rows, W), F32))
        out_specs.append(row0)
    tile = lambda dt=F32, rows=tt, cols=W: pltpu.VMEM((rows, cols), dt)
    scratch = [tile(rows=tt + SUBLANE), tile(rows=tt + SUBLANE), tile(rows=tt + SUBLANE),
               tile(rows=tt + SUBLANE, cols=LANE), tile(BF16, cols=LANE), tile(cols=2 * W),
               tile(), tile(), tile(), tile(), tile(), tile(), tile(BF16, cols=2 * W), tile(),
               tile(), tile(), tile(rows=CHUNK * CHUNK_GROUP), pltpu.VMEM((N_PAIR, PAIR, PAIR), F32)]
    if has_vmix:
        scratch.append(tile())
    outs = pl.pallas_call(
        functools.partial(_rwkv_kernel, has_vmix),
        grid=(nb, nt),
        in_specs=specs,
        out_specs=out_specs,
        out_shape=out_shape,
        scratch_shapes=scratch,
        compiler_params=_params(2),
        name="rwkv7_vmix" if has_vmix else "rwkv7",
    )(*args)
    return (outs[0], v_first) if has_vmix else (outs[0], outs[1])


def _merge_kernel(ga_ref, gb_ref, gc_ref, gd_ref, ma_ref, mb_ref, mc_ref, md_ref,
                  wa_ref, wb_ref, wc_ref, wd_ref, o_ref):
    acc = None
    for g_ref, m_ref, w_ref in ((ga_ref, ma_ref, wa_ref), (gb_ref, mb_ref, wb_ref),
                                (gc_ref, mc_ref, wc_ref), (gd_ref, md_ref, wd_ref)):
        term = _sigmoid(m_ref[...]) * _dot(g_ref[...], w_ref[...])
        acc = term if acc is None else acc + term
    o_ref[...] = acc.astype(BF16)


def _merge(gs, z, ws, layer):
    m = gs[0].shape[0]
    tm = min(512, m)
    tn = 1024
    g_spec = pl.BlockSpec((tm, W), lambda j, i: (i, 0))
    gate = lambda br: pl.BlockSpec((tm, tn), lambda j, i: (i, N_STREAM + br * (D_MODEL // tn) + j))
    w_spec = pl.BlockSpec((None, W, tn), lambda j, i: (layer, 0, j))
    return pl.pallas_call(
        _merge_kernel,
        grid=(D_MODEL // tn, m // tm),
        in_specs=[g_spec] * 4 + [gate(br) for br in range(N_BRANCH)] + [w_spec] * 4,
        out_specs=pl.BlockSpec((tm, tn), lambda j, i: (i, j)),
        out_shape=jax.ShapeDtypeStruct((m, D_MODEL), BF16),
        compiler_params=_params(2),
        name="merge",
    )(*gs, z, z, z, z, *ws)


def _resid_kernel(final, x_ref, m_ref, w_ref, fg_ref, o_ref):
    o_ref[...] = x_ref[...] + _dot(m_ref[...], w_ref[...])
    if final:
        fg = fg_ref[...]

        def blk(r0):
            x = o_ref[pl.ds(r0, 32), :]
            ms = jnp.mean(x * x, axis=-1, keepdims=True)
            o_ref[pl.ds(r0, 32), :] = x * lax.rsqrt(ms + NORM_EPS) * fg
        _for_blocks(o_ref.shape[0], 32, blk)


def _resid(x2, merged, w_o, final_g, layer, final):
    m = x2.shape[0]
    tm = min(512, m)
    return pl.pallas_call(
        functools.partial(_resid_kernel, final),
        grid=(m // tm,),
        in_specs=[pl.BlockSpec((tm, D_MODEL), lambda i: (i, 0)),
                  pl.BlockSpec((tm, D_MODEL), lambda i: (i, 0)),
                  pl.BlockSpec((None, D_MODEL, D_MODEL), lambda i: (layer, 0, 0)),
                  pl.BlockSpec((1, D_MODEL), lambda i: (0, 0))],
        out_specs=pl.BlockSpec((tm, D_MODEL), lambda i: (i, 0)),
        out_shape=jax.ShapeDtypeStruct((m, D_MODEL), F32),
        compiler_params=_params(1),
        name="resid_final" if final else "resid",
    )(x2, merged, w_o, final_g)


def _block_diag(blocks):
    n, c, d = blocks.shape
    eye = jnp.eye(n, dtype=blocks.dtype)
    return (eye[:, None, :, None] * blocks[:, :, None, :]).reshape(n * c, n * d)


def _row(v):
    return v.reshape(1, -1).astype(F32)


def kernel(x, norm_g, w_in, conv_a_w, conv_a_b, lru_wr, lru_br, lru_wi, lru_bi, lru_lambda, w_out_a, conv_b_w, conv_b_b, ln_b_g, ln_b_b, w_out_b, pool_w, pool_scale, w_out_c, mu_rkv, mu_wa, w0, w2, a0, a2, k_k, k_a, r_k, lnx_g, lnx_b, v0, v1, v2, w_out_d, w_o, final_g):
    nb, seq, _ = x.shape
    depth = norm_g.shape[0]
    nt = seq // TIME_TILE
    x2 = x.reshape(nb * seq, D_MODEL)
    n_main = N_STREAM * W
    n_wa = 2 * LOW_RANK

    lane = jnp.arange(LANE)
    head_ones = (lane[:, None] // HEAD == lane[None, :] // HEAD).astype(BF16)
    ee = jnp.concatenate([head_ones, head_ones], axis=0)
    tri = (jnp.arange(CHUNK)[:, None] >= jnp.arange(CHUNK)[None, :]).astype(BF16)
    tri3 = jnp.concatenate([tri, tri, tri], axis=1)

    w_in_b = w_in.astype(BF16)
    w_outs = tuple(w.astype(BF16) for w in (w_out_a, w_out_b, w_out_c, w_out_d))
    w_o_b = w_o.astype(BF16)
    pair_diag = jax.vmap(_block_diag)

    v_first = None
    for l in range(depth):
        z, zwa = _inproj(x2, _row(norm_g[l]), w_in_b, l)

        blk = lru_wr.shape[-1]
        w_gate = jnp.concatenate(
            [pair_diag(lru_wr[l].reshape(LRU_BLOCKS // 2, 2, blk, blk)),
             pair_diag(lru_wi[l].reshape(LRU_BLOCKS // 2, 2, blk, blk))], axis=2).astype(BF16)
        b_gate = jnp.concatenate([lru_br[l], lru_bi[l]]).reshape(1, -1)
        g_a = _lru(z, nb, nt, conv_a_w[l], _row(conv_a_b[l]), w_gate, b_gate, _row(lru_lambda[l]))
        g_b = _conf(z, nb, nt, jnp.repeat(conv_b_w[l], SUBLANE, axis=0), _row(conv_b_b[l]),
                    _row(ln_b_g[l]), _row(ln_b_b[l]))
        g_c = _pool(z, nb, nt, pool_w[l].astype(BF16), _row(pool_scale[l]))

        zero = jnp.zeros((LOW_RANK, W), F32)
        w_wa2 = jnp.concatenate([jnp.concatenate([w2[l], zero], axis=1),
                                 jnp.concatenate([zero, a2[l]], axis=1)], axis=0).astype(BF16)
        params = (mu_rkv[l], _row(mu_wa[l]), _row(w0[l]), w_wa2, _row(a0[l]), _row(k_k[l]),
                  _row(k_a[l]), _row(r_k[l]), _row(lnx_g[l]), _row(lnx_b[l]), ee, tri3)
        vmix = None
        if l > 0:
            v1p = jnp.pad(v1[l - 1], ((0, 0), (0, LANE - VMIX_RANK))).astype(BF16)
            v2p = jnp.pad(v2[l - 1], ((0, LANE - VMIX_RANK), (0, 0))).astype(BF16)
            vmix = (_row(v0[l - 1]), v1p, v2p)
        g_d, v_l = _rwkv(z, zwa, nb, nt, params, v_first, vmix)
        if l == 0:
            v_first = v_l

        merged = _merge((g_a, g_b, g_c, g_d), z, w_outs, l)
        x2 = _resid(x2, merged, w_o_b, _row(final_g), l, l == depth - 1)
    return x2.reshape(nb, seq, D_MODEL)
```

```python
import functools

import jax
import jax.numpy as jnp
from jax import lax
from jax.experimental import pallas as pl
from jax.experimental.pallas import tpu as pltpu

F32 = jnp.float32
BF16 = jnp.bfloat16

D_MODEL = 2048
W = D_MODEL // 2
N_STREAM = 11
N_BRANCH = 4
LRU_BLOCKS = 8
LRU_CONV = 4
LRU_C = 8.0
CONF_WIDTH = 31
POOL_WINDOWS = (2, 4, 8, 16)
POOL_GROUP = W // len(POOL_WINDOWS)
HEAD = 64
LOW_RANK = 64
VMIX_RANK = 32
NORM_EPS = 1e-6
LN_EPS = 1e-5
GN_EPS = 64e-5

LANE = 128
SUBLANE = 8
CHUNK = 64
CHUNK_GROUP = 2
PAIR = 2 * HEAD
N_PAIR = W // PAIR
VMEM_LIMIT = 56 * 1024 * 1024

TIME_TILE = 256
ROW_BLOCK = 16


def _dot(a, b):
    return jnp.dot(a, b, preferred_element_type=F32)


def _dot_nt(a, b):
    return lax.dot_general(a, b, (((1,), (1,)), ((), ())), preferred_element_type=F32)


def _sigmoid(x):
    return 1.0 / (1.0 + jnp.exp(-x))


def _silu(x):
    return x * _sigmoid(x)


def _softplus(x):
    return jnp.maximum(x, 0.0) + jnp.log1p(jnp.exp(-jnp.abs(x)))


def _for_blocks(n_rows, rb, fn, unroll=1):
    def body(i, carry):
        fn(pl.multiple_of(i * rb, rb))
        return carry
    lax.fori_loop(0, n_rows // rb, body, 0, unroll=unroll)


def _params(n_grid):
    return pltpu.CompilerParams(dimension_semantics=("arbitrary",) * n_grid,
                                vmem_limit_bytes=VMEM_LIMIT)


def _shift_history(t, ext_ref, n_hist, n_rows):
    @pl.when(t == 0)
    def _():
        ext_ref[0:n_hist, :] = jnp.zeros((n_hist, ext_ref.shape[1]), ext_ref.dtype)

    @pl.when(t > 0)
    def _():
        ext_ref[0:n_hist, :] = ext_ref[n_rows:n_rows + n_hist, :]


def _inproj_kernel(x_ref, g_ref, w_ref, wwa_ref, z_ref, zwa_ref, u_ref):
    tm = x_ref.shape[0]

    @pl.when(pl.program_id(1) == 0)
    def _():
        g = g_ref[...]

        def blk(r0):
            x = x_ref[pl.ds(r0, 32), :]
            ms = jnp.mean(x * x, axis=-1, keepdims=True)
            u_ref[pl.ds(r0, 32), :] = (x * lax.rsqrt(ms + NORM_EPS) * g).astype(BF16)
        _for_blocks(tm, 32, blk)
        zwa_ref[...] = _dot(u_ref[...], wwa_ref[0])

    z_ref[...] = _dot(u_ref[...], w_ref[0])


def _inproj(x2, g, w_in, layer):
    m = x2.shape[0]
    n_main = N_STREAM * W
    n_wa = 2 * LOW_RANK
    n = w_in.shape[2] - n_wa
    tm = min(1024, m)
    tn = 1024

    def w_window(i, j):
        col = j * tn + jnp.where(j >= N_STREAM, n_wa, 0)
        return (layer, 0, pl.multiple_of(col, LANE))

    return pl.pallas_call(
        _inproj_kernel,
        grid=(m // tm, n // tn),
        in_specs=[
            pl.BlockSpec((tm, D_MODEL), lambda i, j: (i, 0)),
            pl.BlockSpec((1, D_MODEL), lambda i, j: (0, 0)),
            pl.BlockSpec((pl.Element(1), pl.Element(D_MODEL), pl.Element(tn)), w_window),
            pl.BlockSpec((pl.Element(1), pl.Element(D_MODEL), pl.Element(LANE)),
                         lambda i, j: (layer, 0, n_main)),
        ],
        out_specs=[
            pl.BlockSpec((tm, tn), lambda i, j: (i, j)),
            pl.BlockSpec((tm, LANE), lambda i, j: (i, 0)),
        ],
        out_shape=[jax.ShapeDtypeStruct((m, n), F32), jax.ShapeDtypeStruct((m, LANE), F32)],
        scratch_shapes=[pltpu.VMEM((tm, D_MODEL), BF16)],
        compiler_params=_params(2),
        name="inproj",
    )(x2, g, w_in, w_in)


def _lru_kernel(ax_ref, az_ref, cw_ref, cb_ref, wg_ref, bg_ref, lam_ref, o_ref,
                xe_ref, xc_ref, gt_ref, h_ref):
    t = pl.program_id(1)
    tt = ax_ref.shape[0]
    rb = ROW_BLOCK
    hist = SUBLANE
    _shift_history(t, xe_ref, hist, tt)

    @pl.when(t == 0)
    def _():
        h_ref[...] = jnp.zeros(h_ref.shape, F32)

    xe_ref[hist:hist + tt, :] = ax_ref[...]
    cw = cw_ref[...]
    cb = cb_ref[...]

    def conv_blk(r0):
        win = xe_ref[pl.ds(r0, rb + hist), :]
        acc = cb + cw[LRU_CONV - 1:LRU_CONV] * win[hist:hist + rb]
        for j in range(LRU_CONV - 1):
            off = hist - (LRU_CONV - 1) + j
            acc = acc + cw[j:j + 1] * win[off:off + rb]
        xc_ref[pl.ds(r0, rb), :] = acc
    _for_blocks(tt, rb, conv_blk)

    pw = wg_ref.shape[1]
    for q in range(W // pw):
        cols = slice(q * pw, (q + 1) * pw)
        icols = slice(W + q * pw, W + (q + 1) * pw)
        out = _dot(xc_ref[:, cols].astype(BF16), wg_ref[q])
        gt_ref[:, cols] = out[:, :pw] + bg_ref[:, cols]
        gt_ref[:, icols] = out[:, pw:] + bg_ref[:, icols]

    neg_c_softplus = -LRU_C * _softplus(-lam_ref[...])
    row = lax.broadcasted_iota(jnp.int32, (rb, W), 0) & (SUBLANE - 1)

    def scan_blk(i, h_prev):
        r0 = pl.multiple_of(i * rb, rb)
        xc = xc_ref[pl.ds(r0, rb), :]
        g = gt_ref[pl.ds(r0, rb), :]
        r = _sigmoid(g[:, :W])
        ig = _sigmoid(g[:, W:])
        log_a = neg_c_softplus * r
        a = jnp.exp(log_a)
        u = jnp.sqrt(-jnp.tanh(log_a) * (1.0 + a * a)) * (ig * xc)
        for s in (1, 2, 4):
            keep = row >= s
            a_s = jnp.where(keep, pltpu.roll(a, s, axis=0), 1.0)
            u_s = jnp.where(keep, pltpu.roll(u, s, axis=0), 0.0)
            u = u + a * u_s
            a = a * a_s
        hs = []
        for q in range(rb // SUBLANE):
            sl = slice(q * SUBLANE, (q + 1) * SUBLANE)
            h = u[sl] + a[sl] * h_prev
            h_prev = jnp.broadcast_to(h[SUBLANE - 1:SUBLANE], (SUBLANE, W))
            hs.append(h)
        h = jnp.concatenate(hs, axis=0)
        o_ref[pl.ds(r0, rb), :] = (h * _silu(az_ref[pl.ds(r0, rb), :])).astype(BF16)
        return h_prev

    h_ref[...] = lax.fori_loop(0, tt // rb, scan_blk, h_ref[...])


def _lru(z, nb, nt, cw, cb, wg, bg, lam):
    tt = TIME_TILE
    row = lambda c: pl.BlockSpec((tt, W), lambda b, t: (b * nt + t, c))
    full = lambda a: pl.BlockSpec(a.shape, lambda b, t: (0,) * a.ndim)
    return pl.pallas_call(
        _lru_kernel,
        grid=(nb, nt),
        in_specs=[row(0), row(1), full(cw), full(cb), full(wg), full(bg), full(lam)],
        out_specs=pl.BlockSpec((tt, W), lambda b, t: (b * nt + t, 0)),
        out_shape=jax.ShapeDtypeStruct((nb * nt * tt, W), BF16),
        scratch_shapes=[pltpu.VMEM((tt + SUBLANE, W), F32), pltpu.VMEM((tt, W), F32),
                        pltpu.VMEM((tt, 2 * W), F32), pltpu.VMEM((SUBLANE, W), F32)],
        compiler_params=_params(2),
        name="rglru",
    )(z, z, cw, cb, wg, bg, lam)


CONF_HIST = 32


def _conf_kernel(v_ref, glu_ref, bz_ref, cw_ref, cb_ref, lg_ref, lb_ref, o_ref, he_ref, cv_ref):
    t = pl.program_id(1)
    tt = v_ref.shape[0]
    rb = 2 * ROW_BLOCK
    n_tiles = W // LANE
    for c in range(n_tiles):
        _shift_history(t, he_ref.at[0, c], CONF_HIST, tt)

    def glu_blk(r0):
        h = v_ref[pl.ds(r0, rb), :] * _sigmoid(glu_ref[pl.ds(r0, rb), :])
        for c in range(n_tiles):
            he_ref[0, c, pl.ds(CONF_HIST + r0, rb), :] = h[:, c * LANE:(c + 1) * LANE]
    _for_blocks(tt, rb, glu_blk)
    n_shifted = tt + CONF_HIST - SUBLANE
    for s in range(1, SUBLANE):
        for c in range(n_tiles):
            he_ref[s, c, 0:n_shifted, :] = he_ref[0, c, s:s + n_shifted, :]

    for c in range(n_tiles):
        lanes = slice(c * LANE, (c + 1) * LANE)
        taps = [cw_ref[j * SUBLANE:(j + 1) * SUBLANE, lanes] for j in range(CONF_WIDTH)]
        bias = jnp.broadcast_to(cb_ref[:, lanes], (SUBLANE, LANE))

        def conv_rows(i, carry, c=c, lanes=lanes, taps=taps, bias=bias):
            r = pl.multiple_of(i * SUBLANE, SUBLANE)
            parts = [bias, None, None, None]
            for j in range(CONF_WIDTH):
                off = CONF_HIST - (CONF_WIDTH - 1) + j
                s = off % SUBLANE
                term = taps[j] * he_ref[s, c, pl.ds(r + (off - s), SUBLANE), :]
                parts[j % 4] = term if parts[j % 4] is None else parts[j % 4] + term
            cv_ref[pl.ds(r, SUBLANE), lanes] = (parts[0] + parts[1]) + (parts[2] + parts[3])
            return carry
        lax.fori_loop(0, tt // SUBLANE, conv_rows, 0, unroll=2)

    lg = lg_ref[...]
    lb = lb_ref[...]

    def norm_blk(r0):
        acc = cv_ref[pl.ds(r0, rb), :]
        mu = jnp.mean(acc, axis=-1, keepdims=True)
        d = acc - mu
        var = jnp.mean(d * d, axis=-1, keepdims=True)
        y = d * lax.rsqrt(var + LN_EPS) * lg + lb
        o_ref[pl.ds(r0, rb), :] = (_silu(y) * _silu(bz_ref[pl.ds(r0, rb), :])).astype(BF16)
    _for_blocks(tt, rb, norm_blk, unroll=2)


def _conf(z, nb, nt, cw, cb, lg, lb):
    tt = TIME_TILE
    row = lambda c: pl.BlockSpec((tt, W), lambda b, t: (b * nt + t, c))
    full = lambda a: pl.BlockSpec(a.shape, lambda b, t: (0,) * a.ndim)
    return pl.pallas_call(
        _conf_kernel,
        grid=(nb, nt),
        in_specs=[row(2), row(3), row(4), full(cw), full(cb), full(lg), full(lb)],
        out_specs=pl.BlockSpec((tt, W), lambda b, t: (b * nt + t, 0)),
        out_shape=jax.ShapeDtypeStruct((nb * nt * tt, W), BF16),
        scratch_shapes=[pltpu.VMEM((SUBLANE, W // LANE, tt + CONF_HIST, LANE), F32),
                        pltpu.VMEM((tt, W), F32)],
        compiler_params=_params(2),
        name="conformer",
    )(z, z, z, cw, cb, lg, lb)


POOL_HIST = 16


def _pool_kernel(cx_ref, cz_ref, pw_ref, ps_ref, o_ref, xe_ref, p_ref, y_ref):
    t = pl.program_id(1)
    tt = cx_ref.shape[0]
    rb = ROW_BLOCK
    _shift_history(t, xe_ref, POOL_HIST, tt)
    xe_ref[POOL_HIST:POOL_HIST + tt, :] = cx_ref[...]
    t0 = t * tt

    def pool_blk(r0):
        win = xe_ref[pl.ds(r0, rb + POOL_HIST), :]
        count = (lax.broadcasted_iota(jnp.int32, (rb, POOL_GROUP), 0) + (t0 + r0 + 1)).astype(F32)
        for gi, w in enumerate(POOL_WINDOWS):
            lanes = slice(gi * POOL_GROUP, (gi + 1) * POOL_GROUP)
            wg = win[:, lanes]
            xg = wg[POOL_HIST:POOL_HIST + rb]
            s = xg
            for j in range(1, w):
                s = s + wg[POOL_HIST - j:POOL_HIST - j + rb]
            p_ref[pl.ds(r0, rb), lanes] = (s / jnp.minimum(count, float(w)) - xg).astype(BF16)
    _for_blocks(tt, rb, pool_blk)

    for gi in range(len(POOL_WINDOWS)):
        lanes = slice(gi * POOL_GROUP, (gi + 1) * POOL_GROUP)
        y_ref[:, lanes] = _dot(p_ref[:, lanes], pw_ref[gi])

    ps = ps_ref[...]

    def out_blk(r0):
        o_ref[pl.ds(r0, rb), :] = (
            y_ref[pl.ds(r0, rb), :] * ps * _silu(cz_ref[pl.ds(r0, rb), :])).astype(BF16)
    _for_blocks(tt, rb, out_blk)


def _pool(z, nb, nt, pw, ps):
    tt = TIME_TILE
    row = lambda c: pl.BlockSpec((tt, W), lambda b, t: (b * nt + t, c))
    full = lambda a: pl.BlockSpec(a.shape, lambda b, t: (0,) * a.ndim)
    return pl.pallas_call(
        _pool_kernel,
        grid=(nb, nt),
        in_specs=[row(5), row(6), full(pw), full(ps)],
        out_specs=pl.BlockSpec((tt, W), lambda b, t: (b * nt + t, 0)),
        out_shape=jax.ShapeDtypeStruct((nb * nt * tt, W), BF16),
        scratch_shapes=[pltpu.VMEM((tt + POOL_HIST, W), F32), pltpu.VMEM((tt, W), BF16),
                        pltpu.VMEM((tt, W), F32)],
        compiler_params=_params(2),
        name="pool",
    )(z, z, pw, ps)


def _split2(x):
    hi = x.astype(BF16)
    lo = (x - hi.astype(F32)).astype(BF16)
    return hi, lo


def _store_split(q_ref, r0, rb, x):
    hi, lo = _split2(x)
    for c in range(W // LANE):
        q_ref[pl.ds(r0, rb), 2 * c * LANE:(2 * c + 1) * LANE] = hi[:, c * LANE:(c + 1) * LANE]
        q_ref[pl.ds(r0, rb), (2 * c + 1) * LANE:(2 * c + 2) * LANE] = lo[:, c * LANE:(c + 1) * LANE]


def _head_sums(q_ref, ee_ref, out_ref):
    for c in range(W // LANE):
        out_ref[:, c * LANE:(c + 1) * LANE] = _dot(q_ref[:, 2 * c * LANE:(2 * c + 2) * LANE], ee_ref[...])


def _unit_lower_inverses(lmats, masks):
    eye, diag16, off32, off64 = masks
    n = lmats[0].shape[0]
    ds = [jnp.where(diag16, m, 0.0) for m in lmats]
    ts = [eye + d for d in ds]
    ps = [d.astype(BF16) for d in ds]
    ps = [_dot(p, p).astype(BF16) for p in ps]
    for _ in range(2):
        both = [_dot(jnp.concatenate([p, t.astype(BF16)], axis=0), p) for p, t in zip(ps, ts)]
        ts = [t + b[n:] for t, b in zip(ts, both)]
        ps = [b[:n].astype(BF16) for b in both]
    ts = [t + _dot(t.astype(BF16), p) for t, p in zip(ts, ps)]
    for off in (off32, off64):
        los = [jnp.where(off, m, 0.0).astype(BF16) for m in lmats]
        tbs = [t.astype(BF16) for t in ts]
        xs = [_dot(lo, tb).astype(BF16) for lo, tb in zip(los, tbs)]
        ts = [t + _dot(tb, x) for t, tb, x in zip(ts, tbs, xs)]
    return ts


def _rwkv_kernel(has_vmix, *refs):
    if has_vmix:
        (pr_ref, pk_ref, pv_ref, dz_ref, zwa_ref, vf_ref, mu_ref, muwa_ref, w0_ref, wwa_ref, a0_ref,
         kk_ref, ka_ref, rk_ref, lg_ref, lb_ref, ee_ref, tri_ref, v0_ref, v1_ref, v2_ref,
         o_ref,
         pe_r, pe_k, pe_v, pe_wa, twa_s, wa_s, r_s, k_s, v_s, lw_s, kkn_s, be_s, q_s, ss_s,
         bs_s, y_s, cl_s, z_s, vm_s) = refs
        vout_ref = None
    else:
        (pr_ref, pk_ref, pv_ref, dz_ref, zwa_ref, mu_ref, muwa_ref, w0_ref, wwa_ref, a0_ref,
         kk_ref, ka_ref, rk_ref, lg_ref, lb_ref, ee_ref, tri_ref,
         o_ref, vout_ref,
         pe_r, pe_k, pe_v, pe_wa, twa_s, wa_s, r_s, k_s, v_s, lw_s, kkn_s, be_s, q_s, ss_s,
         bs_s, y_s, cl_s, z_s) = refs
    t = pl.program_id(1)
    tt = pr_ref.shape[0]
    rb = ROW_BLOCK
    hist = SUBLANE

    for ext in (pe_r, pe_k, pe_v, pe_wa):
        _shift_history(t, ext, hist, tt)

    @pl.when(t == 0)
    def _():
        z_s[...] = jnp.zeros(z_s.shape, F32)

    pe_r[hist:hist + tt, :] = pr_ref[...]
    pe_k[hist:hist + tt, :] = pk_ref[...]
    pe_v[hist:hist + tt, :] = pv_ref[...]
    pe_wa[hist:hist + tt, :] = zwa_ref[...]

    muwa = muwa_ref[...]
    lane_wa = lax.broadcasted_iota(jnp.int32, (rb, LANE), 1)

    def wa_blk(r0):
        win = pe_wa[pl.ds(r0, rb + hist), :]
        cur = win[hist:hist + rb]
        prev = win[hist - 1:hist - 1 + rb]
        s = cur + (prev - cur) * muwa
        twa_s[pl.ds(r0, rb), :] = jnp.where(lane_wa < LOW_RANK, jnp.tanh(s), s).astype(BF16)
    _for_blocks(tt, rb, wa_blk)
    wa_s[...] = _dot(twa_s[...], wwa_ref[...])

    k_k = kk_ref[...]

    def shift_blk(r0):
        for ext, idx, dst in ((pe_r, 0, r_s), (pe_k, 1, k_s), (pe_v, 2, v_s)):
            win = ext[pl.ds(r0, rb + hist), :]
            cur = win[hist:hist + rb]
            prev = win[hist - 1:hist - 1 + rb]
            dst[pl.ds(r0, rb), :] = cur + (prev - cur) * mu_ref[idx:idx + 1, :]
        kkr = k_s[pl.ds(r0, rb), :] * k_k
        _store_split(q_s, r0, rb, kkr * kkr)
    _for_blocks(tt, rb, shift_blk)
    _head_sums(q_s, ee_ref, ss_s)

    if has_vmix:
        vm_s[...] = _dot(_dot(v_s[...].astype(BF16), v1_ref[...]).astype(BF16), v2_ref[...])

    w0 = w0_ref[...]
    a0 = a0_ref[...]
    k_a = ka_ref[...]
    r_k = rk_ref[...]

    def prep_blk(r0):
        rows = pl.ds(r0, rb)
        wa = wa_s[rows, :]
        w_log = -_softplus(-(w0 + wa[:, :W])) - 0.5
        lw_s[rows, :] = -jnp.exp(w_log)
        a = _sigmoid(a0 + wa[:, W:])
        k = k_s[rows, :]
        kkn = (k * k_k) * lax.rsqrt(jnp.maximum(ss_s[rows, :], 1e-24))
        kkn_s[rows, :] = kkn
        be_s[rows, :] = kkn * a
        k_eff = k * (1.0 + (a - 1.0) * k_a)
        k_s[rows, :] = k_eff
        v = v_s[rows, :]
        if has_vmix:
            v = v + (vf_ref[rows, :] - v) * _sigmoid(v0_ref[...] + vm_s[rows, :])
            v_s[rows, :] = v
        else:
            vout_ref[rows, :] = v
        _store_split(q_s, r0, rb, r_s[rows, :] * k_eff * r_k)
    _for_blocks(tt, rb, prep_blk)
    _head_sums(q_s, ee_ref, bs_s)

    ri = lax.broadcasted_iota(jnp.int32, (PAIR, PAIR), 0)
    ci = lax.broadcasted_iota(jnp.int32, (PAIR, PAIR), 1)
    strict = ri > ci
    incl = ri >= ci
    eye = jnp.where(ri == ci, 1.0, 0.0).astype(F32)
    masks = (eye, (ri >> 4) == (ci >> 4),
             ((ri >> 5) == (ci >> 5)) & ((ri >> 4) != (ci >> 4)),
             ((ri >> 6) == (ci >> 6)) & ((ri >> 5) != (ci >> 5)))
    lane_c = lax.broadcasted_iota(jnp.int32, (CHUNK, PAIR), 1)
    first_head = lane_c < HEAD

    def stack(x):
        return jnp.concatenate([jnp.where(first_head, x, 0.0), jnp.where(first_head, 0.0, x)], axis=0)

    def chunk_body(c, carry):
        a2s, r2s, v2bs, bkts, wcols, gs = [], [], [], [], [], []
        for q in range(CHUNK_GROUP):
            rows = pl.ds(pl.multiple_of((c * CHUNK_GROUP + q) * CHUNK, CHUNK), CHUNK)
            crow = slice(q * CHUNK, (q + 1) * CHUNK)
            lw = lw_s[rows, :]
            h1 = lw.astype(BF16)
            r1 = lw - h1.astype(F32)
            h2 = r1.astype(BF16)
            h3 = (r1 - h2.astype(F32)).astype(BF16)
            cl_s[crow, :] = _dot(tri_ref[...], jnp.concatenate([h1, h2, h3], axis=0))
            for p in range(N_PAIR):
                lanes = slice(p * PAIR, (p + 1) * PAIR)
                cl = cl_s[crow, lanes]
                e_pos = jnp.exp(cl)
                e_neg = jnp.exp(-cl)
                e_ex = jnp.exp(cl - lw_s[rows, lanes])
                wc = e_pos[CHUNK - 1:CHUNK]
                kt = k_s[rows, lanes] * e_neg
                bt = be_s[rows, lanes] * e_neg
                a2 = stack(-kkn_s[rows, lanes] * e_ex)
                r2 = stack(r_s[rows, lanes] * e_pos)
                gs.append(_dot_nt(jnp.concatenate([a2, r2], axis=0).astype(BF16),
                                  jnp.concatenate([stack(bt), stack(kt)], axis=0).astype(BF16)))
                a2s.append(a2)
                r2s.append(r2)
                v2bs.append(stack(v_s[rows, lanes]).astype(BF16))
                bkts.append(jnp.concatenate([jnp.transpose(stack(bt * wc)),
                                             jnp.transpose(stack(kt * wc))], axis=1).astype(BF16))
                wcols.append(jnp.transpose(jnp.broadcast_to(wc, (PAIR, PAIR))))
        labs = [jnp.where(strict, g[:PAIR, :PAIR], 0.0) for g in gs]
        laks = [jnp.where(strict, g[:PAIR, PAIR:], 0.0).astype(BF16) for g in gs]
        ylhs = [jnp.concatenate([r2, jnp.where(incl, g[PAIR:, :PAIR], 0.0),
                                 jnp.where(incl, g[PAIR:, PAIR:], 0.0)], axis=1).astype(BF16)
                for r2, g in zip(r2s, gs)]
        lvs = [_dot(lak, v2b) for lak, v2b in zip(laks, v2bs)]
        tinvs = _unit_lower_inverses(labs, masks)
        pms = [_dot(t.astype(BF16), jnp.concatenate([a2, lv], axis=1).astype(BF16))
               for t, a2, lv in zip(tinvs, a2s, lvs)]
        for q in range(CHUNK_GROUP):
            rows = pl.ds(pl.multiple_of((c * CHUNK_GROUP + q) * CHUNK, CHUNK), CHUNK)
            ids = [q * N_PAIR + p for p in range(N_PAIR)]
            zs = [z_s[p] for p in range(N_PAIR)]
            zbs = [z.astype(BF16) for z in zs]
            ubs = [(_dot(pms[i][:, :PAIR].astype(BF16), zb) + pms[i][:, PAIR:]).astype(BF16)
                   for i, zb in zip(ids, zbs)]
            for p, i in enumerate(ids):
                y = _dot(ylhs[i], jnp.concatenate([zbs[p], ubs[p], v2bs[i]], axis=0))
                y_s[rows, p * PAIR:(p + 1) * PAIR] = y[:CHUNK] + y[CHUNK:]
            for p, i in enumerate(ids):
                z_s[p] = wcols[i] * zs[p] + _dot(bkts[i], jnp.concatenate([ubs[p], v2bs[i]], axis=0))
        return carry
    lax.fori_loop(0, tt // (CHUNK * CHUNK_GROUP), chunk_body, 0)

    inv_n = 1.0 / HEAD

    def mean_blk(r0):
        _store_split(q_s, r0, rb, y_s[pl.ds(r0, rb), :])
    _for_blocks(tt, rb, mean_blk)
    _head_sums(q_s, ee_ref, ss_s)

    def center_blk(r0):
        rows = pl.ds(r0, rb)
        d = y_s[rows, :] - ss_s[rows, :] * inv_n
        y_s[rows, :] = d
        _store_split(q_s, r0, rb, d * d)
    _for_blocks(tt, rb, center_blk)
    _head_sums(q_s, ee_ref, ss_s)

    lg = lg_ref[...]
    lb = lb_ref[...]

    def out_blk(r0):
        rows = pl.ds(r0, rb)
        y = y_s[rows, :] * lax.rsqrt(ss_s[rows, :] * inv_n + GN_EPS) * lg + lb
        y = y + bs_s[rows, :] * v_s[rows, :]
        o_ref[rows, :] = (y * _silu(dz_ref[rows, :])).astype(BF16)
    _for_blocks(tt, rb, out_blk)


def _rwkv(z, zwa, nb, nt, params, v_first, vmix):
    tt = TIME_TILE
    has_vmix = vmix is not None
    row = lambda c: pl.BlockSpec((tt, W), lambda b, t: (b * nt + t, c))
    row0 = pl.BlockSpec((tt, W), lambda b, t: (b * nt + t, 0))
    full = lambda a: pl.BlockSpec(a.shape, lambda b, t: (0,) * a.ndim)
    args = [z, z, z, z, zwa]
    specs = [row(7), row(8), row(9), row(10), pl.BlockSpec((tt, LANE), lambda b, t: (b * nt + t, 0))]
    if has_vmix:
        args.append(v_first)
        specs.append(row0)
    args += list(params)
    specs += [full(a) for a in params]
    if has_vmix:
        args += list(vmix)
        specs += [full(a) for a in vmix]
    n_rows = nb * nt * tt
    out_shape = [jax.ShapeDtypeStruct((n_rows, W), BF16)]
    out_specs = [row0]
    if not has_vmix:
        out_shape.append(jax.ShapeDtypeStruct((n_rows, W), F32))
        out_specs.append(row0)
    tile = lambda dt=F32, rows=tt, cols=W: pltpu.VMEM((rows, cols), dt)
    scratch = [tile(rows=tt + SUBLANE), tile(rows=tt + SUBLANE), tile(rows=tt + SUBLANE),
               tile(rows=tt + SUBLANE, cols=LANE), tile(BF16, cols=LANE), tile(cols=2 * W),
               tile(), tile(), tile(), tile(), tile(), tile(), tile(BF16, cols=2 * W), tile(),
               tile(), tile(), tile(rows=CHUNK * CHUNK_GROUP), pltpu.VMEM((N_PAIR, PAIR, PAIR), F32)]
    if has_vmix:
        scratch.append(tile())
    outs = pl.pallas_call(
        functools.partial(_rwkv_kernel, has_vmix),
        grid=(nb, nt),
        in_specs=specs,
        out_specs=out_specs,
        out_shape=out_shape,
        scratch_shapes=scratch,
        compiler_params=_params(2),
        name="rwkv7_vmix" if has_vmix else "rwkv7",
    )(*args)
    return (outs[0], v_first) if has_vmix else (outs[0], outs[1])


def _merge_kernel(ga_ref, gb_ref, gc_ref, gd_ref, ma_ref, mb_ref, mc_ref, md_ref,
                  wa_ref, wb_ref, wc_ref, wd_ref, o_ref):
    acc = None
    for g_ref, m_ref, w_ref in ((ga_ref, ma_ref, wa_ref), (gb_ref, mb_ref, wb_ref),
                                (gc_ref, mc_ref, wc_ref), (gd_ref, md_ref, wd_ref)):
        term = _sigmoid(m_ref[...]) * _dot(g_ref[...], w_ref[...])
        acc = term if acc is None else acc + term
    o_ref[...] = acc.astype(BF16)


def _merge(gs, z, ws, layer):
    m = gs[0].shape[0]
    tm = min(512, m)
    tn = 1024
    g_spec = pl.BlockSpec((tm, W), lambda j, i: (i, 0))
    gate = lambda br: pl.BlockSpec((tm, tn), lambda j, i: (i, N_STREAM + br * (D_MODEL // tn) + j))
    w_spec = pl.BlockSpec((None, W, tn), lambda j, i: (layer, 0, j))
    return pl.pallas_call(
        _merge_kernel,
        grid=(D_MODEL // tn, m // tm),
        in_specs=[g_spec] * 4 + [gate(br) for br in range(N_BRANCH)] + [w_spec] * 4,
        out_specs=pl.BlockSpec((tm, tn), lambda j, i: (i, j)),
        out_shape=jax.ShapeDtypeStruct((m, D_MODEL), BF16),
        compiler_params=_params(2),
        name="merge",
    )(*gs, z, z, z, z, *ws)


def _merge_resid_kernel(final, x_ref, ga_ref, gb_ref, gc_ref, gd_ref,
                        ma0, ma1, mb0, mb1, mc0, mc1, md0, md1,
                        wa_ref, wb_ref, wc_ref, wd_ref, wo_ref, fg_ref, o_ref, mg_ref):
    half = D_MODEL // 2
    branches = ((ga_ref, (ma0, ma1), wa_ref), (gb_ref, (mb0, mb1), wb_ref),
                (gc_ref, (mc0, mc1), wc_ref), (gd_ref, (md0, md1), wd_ref))
    for h in range(2):
        cols = slice(h * half, (h + 1) * half)
        acc = None
        for g_ref, m_refs, w_ref in branches:
            term = _sigmoid(m_refs[h][...]) * _dot(g_ref[...], w_ref[:, cols])
            acc = term if acc is None else acc + term
        mg_ref[:, cols] = acc.astype(BF16)
    o_ref[...] = x_ref[...] + _dot(mg_ref[...], wo_ref[...])
    if final:
        fg = fg_ref[...]

        def blk(r0):
            x = o_ref[pl.ds(r0, 32), :]
            ms = jnp.mean(x * x, axis=-1, keepdims=True)
            o_ref[pl.ds(r0, 32), :] = x * lax.rsqrt(ms + NORM_EPS) * fg
        _for_blocks(o_ref.shape[0], 32, blk)


def _merge_resid(x2, gs, z, ws, w_o, final_g, layer, final):
    m = x2.shape[0]
    tm = min(128, m)
    half = D_MODEL // 2
    once = pl.Buffered(1)
    x_spec = pl.BlockSpec((tm, D_MODEL), lambda i: (i, 0))
    g_spec = pl.BlockSpec((tm, W), lambda i: (i, 0))
    gate = lambda br, h: pl.BlockSpec((tm, half), lambda i: (i, N_STREAM + 2 * br + h))
    w_spec = pl.BlockSpec((None, W, D_MODEL), lambda i: (layer, 0, 0), pipeline_mode=once)
    wo_spec = pl.BlockSpec((None, D_MODEL, D_MODEL), lambda i: (layer, 0, 0), pipeline_mode=once)
    fg_spec = pl.BlockSpec((1, D_MODEL), lambda i: (0, 0))
    return pl.pallas_call(
        functools.partial(_merge_resid_kernel, final),
        grid=(m // tm,),
        in_specs=([x_spec] + [g_spec] * 4
                  + [gate(br, h) for br in range(N_BRANCH) for h in range(2)]
                  + [w_spec] * 4 + [wo_spec, fg_spec]),
        out_specs=x_spec,
        out_shape=jax.ShapeDtypeStruct((m, D_MODEL), F32),
        scratch_shapes=[pltpu.VMEM((tm, D_MODEL), BF16)],
        compiler_params=_params(1),
        name="merge_resid_final" if final else "merge_resid",
    )(x2, *gs, *([z] * 8), *ws, w_o, final_g)


def _resid_kernel(final, x_ref, m_ref, w_ref, fg_ref, o_ref):
    o_ref[...] = x_ref[...] + _dot(m_ref[...], w_ref[...])
    if final:
        fg = fg_ref[...]

        def blk(r0):
            x = o_ref[pl.ds(r0, 32), :]
            ms = jnp.mean(x * x, axis=-1, keepdims=True)
            o_ref[pl.ds(r0, 32), :] = x * lax.rsqrt(ms + NORM_EPS) * fg
        _for_blocks(o_ref.shape[0], 32, blk)


def _resid(x2, merged, w_o, final_g, layer, final):
    m = x2.shape[0]
    tm = min(512, m)
    return pl.pallas_call(
        functools.partial(_resid_kernel, final),
        grid=(m // tm,),
        in_specs=[pl.BlockSpec((tm, D_MODEL), lambda i: (i, 0)),
                  pl.BlockSpec((tm, D_MODEL), lambda i: (i, 0)),
                  pl.BlockSpec((None, D_MODEL, D_MODEL), lambda i: (layer, 0, 0)),
                  pl.BlockSpec((1, D_MODEL), lambda i: (0, 0))],
        out_specs=pl.BlockSpec((tm, D_MODEL), lambda i: (i, 0)),
        out_shape=jax.ShapeDtypeStruct((m, D_MODEL), F32),
        compiler_params=_params(1),
        name="resid_final" if final else "resid",
    )(x2, merged, w_o, final_g)


def _block_diag(blocks):
    n, c, d = blocks.shape
    eye = jnp.eye(n, dtype=blocks.dtype)
    return (eye[:, None, :, None] * blocks[:, :, None, :]).reshape(n * c, n * d)


def _row(v):
    return v.reshape(1, -1).astype(F32)


def kernel(x, norm_g, w_in, conv_a_w, conv_a_b, lru_wr, lru_br, lru_wi, lru_bi, lru_lambda, w_out_a, conv_b_w, conv_b_b, ln_b_g, ln_b_b, w_out_b, pool_w, pool_scale, w_out_c, mu_rkv, mu_wa, w0, w2, a0, a2, k_k, k_a, r_k, lnx_g, lnx_b, v0, v1, v2, w_out_d, w_o, final_g):
    nb, seq, _ = x.shape
    depth = norm_g.shape[0]
    nt = seq // TIME_TILE
    x2 = x.reshape(nb * seq, D_MODEL)
    n_main = N_STREAM * W
    n_wa = 2 * LOW_RANK

    lane = jnp.arange(LANE)
    head_ones = (lane[:, None] // HEAD == lane[None, :] // HEAD).astype(BF16)
    ee = jnp.concatenate([head_ones, head_ones], axis=0)
    tri = (jnp.arange(CHUNK)[:, None] >= jnp.arange(CHUNK)[None, :]).astype(BF16)
    tri3 = jnp.concatenate([tri, tri, tri], axis=1)

    w_in_b = w_in.astype(BF16)
    w_outs = tuple(w.astype(BF16) for w in (w_out_a, w_out_b, w_out_c, w_out_d))
    w_o_b = w_o.astype(BF16)
    pair_diag = jax.vmap(_block_diag)

    v_first = None
    for l in range(depth):
        z, zwa = _inproj(x2, _row(norm_g[l]), w_in_b, l)

        blk = lru_wr.shape[-1]
        w_gate = jnp.concatenate(
            [pair_diag(lru_wr[l].reshape(LRU_BLOCKS // 2, 2, blk, blk)),
             pair_diag(lru_wi[l].reshape(LRU_BLOCKS // 2, 2, blk, blk))], axis=2).astype(BF16)
        b_gate = jnp.concatenate([lru_br[l], lru_bi[l]]).reshape(1, -1)
        g_a = _lru(z, nb, nt, conv_a_w[l], _row(conv_a_b[l]), w_gate, b_gate, _row(lru_lambda[l]))
        g_b = _conf(z, nb, nt, jnp.repeat(conv_b_w[l], SUBLANE, axis=0), _row(conv_b_b[l]),
                    _row(ln_b_g[l]), _row(ln_b_b[l]))
        g_c = _pool(z, nb, nt, pool_w[l].astype(BF16), _row(pool_scale[l]))

        zero = jnp.zeros((LOW_RANK, W), F32)
        w_wa2 = jnp.concatenate([jnp.concatenate([w2[l], zero], axis=1),
                                 jnp.concatenate([zero, a2[l]], axis=1)], axis=0).astype(BF16)
        params = (mu_rkv[l], _row(mu_wa[l]), _row(w0[l]), w_wa2, _row(a0[l]), _row(k_k[l]),
                  _row(k_a[l]), _row(r_k[l]), _row(lnx_g[l]), _row(lnx_b[l]), ee, tri3)
        vmix = None
        if l > 0:
            v1p = jnp.pad(v1[l - 1], ((0, 0), (0, LANE - VMIX_RANK))).astype(BF16)
            v2p = jnp.pad(v2[l - 1], ((0, LANE - VMIX_RANK), (0, 0))).astype(BF16)
            vmix = (_row(v0[l - 1]), v1p, v2p)
        g_d, v_l = _rwkv(z, zwa, nb, nt, params, v_first, vmix)
        if l == 0:
            v_first = v_l

        x2 = _merge_resid(x2, (g_a, g_b, g_c, g_d), z, w_outs, w_o_b, _row(final_g), l,
                          l == depth - 1)
    return x2.reshape(nb, seq, D_MODEL)
```
